```python
import math
import jax, jax.numpy as jnp
from jax import lax
import numpy as np

D_MODEL = 1024
BATCH = 2
SEQ = 16384
DEPTH = 2

N_A_LAYERS = DEPTH // 2
N_B_LAYERS = DEPTH - N_A_LAYERS
GDN_HEADS = D_MODEL // 128
GDN_DK = 128
GDN_DV = 128
CONV_K = 4
CHUNK = 64
GDN_HK = GDN_HEADS * GDN_DK
GDN_HV = GDN_HEADS * GDN_DV
GDN_IN = 2 * GDN_HK + 2 * GDN_HV + 2 * GDN_HEADS
DIFF_HEADS = D_MODEL // 256
DIFF_DH = 128
DIFF_QW = DIFF_HEADS * 2 * DIFF_DH
DIFF_VW = DIFF_HEADS * 2 * DIFF_DH
ROT_DIM = DIFF_DH // 4
ROPE_THETA = 500000.0
Q_BLOCK = 128
D_FF = 4 * D_MODEL
EPS = 1e-6

kernel_name = "yoco_gdn_diffattn_adaln_trunk"


def rmsnorm(x, g):
    xf = x.astype(jnp.float32)
    y = xf * lax.rsqrt(jnp.mean(xf * xf, axis=-1, keepdims=True) + EPS)
    return (y * g.astype(jnp.float32)).astype(x.dtype)


def modulate(x, g, shift, scale):
    return rmsnorm(x, g) * (1 + scale[:, None, :]) + shift[:, None, :]


def l2norm(x):
    xf = x.astype(jnp.float32)
    return (xf * lax.rsqrt(jnp.sum(xf * xf, axis=-1, keepdims=True) + EPS)).astype(x.dtype)


def causal_conv_silu(x, w):
    K = w.shape[0]
    S = x.shape[1]
    xp = jnp.pad(x, ((0, 0), (K - 1, 0), (0, 0)))
    y = xp[:, 0:S] * w[0]
    for j in range(1, K):
        y = y + xp[:, j:j + S] * w[j]
    return jax.nn.silu(y)


def gated_delta_rule(q, k, v, g, beta):
    f32 = jnp.float32
    B, S, H, DK = q.shape
    DV = v.shape[-1]
    N = S // CHUNK

    def chunks(t):
        return t.astype(f32).reshape(B, N, CHUNK, H, -1).transpose(0, 3, 1, 2, 4)

    qc = chunks(q) * (DK ** -0.5)
    kc = chunks(k)
    vc = chunks(v)
    gc = g.astype(f32).reshape(B, N, CHUNK, H).transpose(0, 3, 1, 2)
    bc = beta.astype(f32).reshape(B, N, CHUNK, H).transpose(0, 3, 1, 2)
    gcum = jnp.cumsum(gc, axis=-1)
    idx = jnp.arange(CHUNK)
    incl = idx[:, None] >= idx[None, :]
    strict = idx[:, None] > idx[None, :]
    gdiff = gcum[..., :, None] - gcum[..., None, :]
    decay = jnp.where(incl, jnp.exp(jnp.where(incl, gdiff, 0.0)), 0.0)
    kb = kc * bc[..., None]
    a_mat = jnp.where(strict, jnp.einsum('bhnik,bhnjk->bhnij', kb, kc) * decay, 0.0)
    eye = jnp.eye(CHUNK, dtype=f32)
    rhs = jnp.concatenate([vc * bc[..., None], kb * jnp.exp(gcum)[..., None]], axis=-1)
    sol = lax.linalg.triangular_solve(eye + a_mat, rhs, left_side=True, lower=True, unit_diagonal=True)
    u = sol[..., :DV]
    w = sol[..., DV:]
    qk = jnp.einsum('bhnik,bhnjk->bhnij', qc, kc) * decay
    qg = qc * jnp.exp(gcum)[..., None]
    kdec = kc * jnp.exp(gcum[..., -1:] - gcum)[..., None]
    glast = jnp.exp(gcum[..., -1])
    xs = tuple(jnp.moveaxis(t, 2, 0) for t in (qg, qk, u, w, kdec, glast))

    def step(state, inp):
        qg_i, qk_i, u_i, w_i, kd_i, gl_i = inp
        v_new = u_i - jnp.einsum('bhck,bhkv->bhcv', w_i, state)
        o_i = jnp.einsum('bhck,bhkv->bhcv', qg_i, state) + jnp.einsum('bhij,bhjv->bhiv', qk_i, v_new)
        state = state * gl_i[..., None, None] + jnp.einsum('bhck,bhcv->bhkv', kd_i, v_new)
        return state, o_i

    s0 = jnp.zeros((B, H, DK, DV), f32)
    _, o = lax.scan(step, s0, xs)
    return o.transpose(1, 0, 3, 2, 4).reshape(B, S, H, DV)


def gated_deltanet(h, w_in, conv_w, a_log, dt_bias, out_g, w_out):
    B, S, _ = h.shape
    proj = h @ w_in
    qkv = causal_conv_silu(proj[..., :2 * GDN_HK + GDN_HV], conv_w)
    q = qkv[..., :GDN_HK].reshape(B, S, GDN_HEADS, GDN_DK)
    k = qkv[..., GDN_HK:2 * GDN_HK].reshape(B, S, GDN_HEADS, GDN_DK)
    v = qkv[..., 2 * GDN_HK:].reshape(B, S, GDN_HEADS, GDN_DV)
    off = 2 * GDN_HK + GDN_HV
    z = proj[..., off:off + GDN_HV].reshape(B, S, GDN_HEADS, GDN_DV)
    a = proj[..., off + GDN_HV:off + GDN_HV + GDN_HEADS]
    b = proj[..., off + GDN_HV + GDN_HEADS:]
    g = -jnp.exp(a_log.astype(jnp.float32)) * jax.nn.softplus(a.astype(jnp.float32) + dt_bias.astype(jnp.float32))
    beta = jax.nn.sigmoid(b.astype(jnp.float32))
    o = gated_delta_rule(l2norm(q), l2norm(k), v, g, beta).astype(h.dtype)
    o = rmsnorm(o, out_g) * jax.nn.silu(z)
    return o.reshape(B, S, GDN_HV) @ w_out


def rope_tables(S):
    pos = jnp.arange(S, dtype=jnp.float32)
    inv_freq = ROPE_THETA ** (-jnp.arange(0, ROT_DIM, 2, dtype=jnp.float32) / ROT_DIM)
    freqs = pos[:, None] * inv_freq[None, :]
    return jnp.cos(freqs), jnp.sin(freqs)


def partial_rope(x, cos, sin):
    half = ROT_DIM // 2
    xf = x.astype(jnp.float32)
    x1 = xf[..., :half]
    x2 = xf[..., half:ROT_DIM]
    c = cos[None, :, None, None, :]
    s = sin[None, :, None, None, :]
    out = jnp.concatenate([x1 * c - x2 * s, x2 * c + x1 * s, xf[..., ROT_DIM:]], axis=-1)
    return out.astype(x.dtype)


def diff_attention(q, k, v, lam):
    B, S, H, _, DH = q.shape
    nb = S // Q_BLOCK
    qb = q.reshape(B, nb, Q_BLOCK, H, 2, DH).transpose(1, 0, 3, 4, 2, 5)
    kt = k.transpose(0, 2, 3, 1, 4)
    vt = v.transpose(0, 2, 1, 3)
    kpos = jnp.arange(S)
    scale = DH ** -0.5

    def block(args):
        i, qi = args
        s = jnp.einsum('bhmqd,bhmkd->bhmqk', qi, kt).astype(jnp.float32) * scale
        qpos = i * Q_BLOCK + jnp.arange(Q_BLOCK)
        s = jnp.where(kpos[None, :] <= qpos[:, None], s, -jnp.inf)
        p = jax.nn.softmax(s, axis=-1)
        a = p[:, :, 0] - lam * p[:, :, 1]
        return jnp.einsum('bhqk,bhkv->bhqv', a.astype(v.dtype), vt)

    o = lax.map(block, (jnp.arange(nb), qb))
    return o.transpose(1, 0, 3, 2, 4).reshape(B, S, H, 2 * DH)


def diff_attn_layer(h, k, v, w_q, lam_params, subln_g, w_out, lam_init, cos, sin):
    B, S, _ = h.shape
    q = (h @ w_q).reshape(B, S, DIFF_HEADS, 2, DIFF_DH)
    q = partial_rope(q, cos, sin)
    lp = lam_params.astype(jnp.float32)
    lam = jnp.exp(jnp.sum(lp[0] * lp[1])) - jnp.exp(jnp.sum(lp[2] * lp[3])) + lam_init
    o = diff_attention(q, k, v, lam)
    o = rmsnorm(o, subln_g) * (1.0 - lam_init)
    return o.reshape(B, S, DIFF_VW) @ w_out


def sqrelu_mlp(h, w1, w2):
    return jnp.square(jax.nn.relu(h @ w1)) @ w2


def setup_inputs(seed: int = 0) -> dict:
    key = jax.random.key(seed)
    ks = jax.random.split(key, 32)
    D = D_MODEL

    def nrm(k, shape, scale):
        return jax.random.normal(k, shape, jnp.float32) * scale

    dt = jnp.exp(jax.random.uniform(ks[8], (N_A_LAYERS, GDN_HEADS), jnp.float32) * (math.log(0.1) - math.log(0.001)) + math.log(0.001))
    return {
        "x": nrm(ks[0], (BATCH, SEQ, D), 1.0),
        "c": nrm(ks[1], (BATCH, D), 1.0),
        "mod_w": nrm(ks[2], (DEPTH, D, 6 * D), 0.5 * D ** -0.5),
        "mod_b": nrm(ks[3], (DEPTH, 6 * D), 0.02),
        "norm_mix_g": 1.0 + nrm(ks[4], (DEPTH, D), 0.02),
        "norm_mlp_g": 1.0 + nrm(ks[5], (DEPTH, D), 0.02),
        "a_w_in": nrm(ks[6], (N_A_LAYERS, D, GDN_IN), D ** -0.5),
        "a_conv_w": nrm(ks[7], (N_A_LAYERS, CONV_K, 2 * GDN_HK + GDN_HV), CONV_K ** -0.5),
        "a_log": jnp.log(jax.random.uniform(ks[9], (N_A_LAYERS, GDN_HEADS), jnp.float32, 1.0, 16.0)),
        "a_dt_bias": dt + jnp.log(-jnp.expm1(-dt)),
        "a_out_norm_g": 1.0 + nrm(ks[10], (N_A_LAYERS, GDN_DV), 0.02),
        "a_w_out": nrm(ks[11], (N_A_LAYERS, GDN_HV, D), GDN_HV ** -0.5),
        "kv_mod_w": nrm(ks[12], (D, 2 * D), 0.5 * D ** -0.5),
        "kv_mod_b": nrm(ks[13], (2 * D,), 0.02),
        "kv_norm_g": 1.0 + nrm(ks[14], (D,), 0.02),
        "kv_w": nrm(ks[15], (D, DIFF_HEADS * 2 * DIFF_DH + DIFF_VW), D ** -0.5),
        "b_w_q": nrm(ks[16], (N_B_LAYERS, D, DIFF_QW), D ** -0.5),
        "b_lambda": nrm(ks[17], (N_B_LAYERS, 4, DIFF_DH), 0.1),
        "b_subln_g": 1.0 + nrm(ks[18], (N_B_LAYERS, 2 * DIFF_DH), 0.02),
        "b_w_out": nrm(ks[19], (N_B_LAYERS, DIFF_VW, D), DIFF_VW ** -0.5),
        "mlp_w1": nrm(ks[20], (DEPTH, D, D_FF), D ** -0.5),
        "mlp_w2": nrm(ks[21], (DEPTH, D_FF, D), D_FF ** -0.5),
        "final_g": 1.0 + nrm(ks[22], (D,), 0.02),
    }


def reference(x, c, mod_w, mod_b, norm_mix_g, norm_mlp_g, a_w_in, a_conv_w, a_log, a_dt_bias, a_out_norm_g, a_w_out,
              kv_mod_w, kv_mod_b, kv_norm_g, kv_w, b_w_q, b_lambda, b_subln_g, b_w_out, mlp_w1, mlp_w2, final_g):
    B, S, D = x.shape
    cs = jax.nn.silu(c)
    cos, sin = rope_tables(S)
    k_sh = None
    v_sh = None
    for l in range(DEPTH):
        mod = cs @ mod_w[l] + mod_b[l]
        sh1, sc1, gt1, sh2, sc2, gt2 = jnp.split(mod, 6, axis=-1)
        if l < N_A_LAYERS:
            h = modulate(x, norm_mix_g[l], sh1, sc1)
            y = gated_deltanet(h, a_w_in[l], a_conv_w[l], a_log[l], a_dt_bias[l], a_out_norm_g[l], a_w_out[l])
        else:
            if l == N_A_LAYERS:
                kv_sh_shift, kv_sh_scale = jnp.split(cs @ kv_mod_w + kv_mod_b, 2, axis=-1)
                kv = modulate(x, kv_norm_g, kv_sh_shift, kv_sh_scale) @ kv_w
                k_sh = partial_rope(kv[..., :DIFF_HEADS * 2 * DIFF_DH].reshape(B, S, DIFF_HEADS, 2, DIFF_DH), cos, sin)
                v_sh = kv[..., DIFF_HEADS * 2 * DIFF_DH:].reshape(B, S, DIFF_HEADS, 2 * DIFF_DH)
            j = l - N_A_LAYERS
            lam_init = 0.8 - 0.6 * math.exp(-0.3 * l)
            h = modulate(x, norm_mix_g[l], sh1, sc1)
            y = diff_attn_layer(h, k_sh, v_sh, b_w_q[j], b_lambda[j], b_subln_g[j], b_w_out[j], lam_init, cos, sin)
        x = x + gt1[:, None, :] * y
        h = modulate(x, norm_mlp_g[l], sh2, sc2)
        x = x + gt2[:, None, :] * sqrelu_mlp(h, mlp_w1[l], mlp_w2[l])
    return rmsnorm(x, final_g)
```

```python
import functools
import math

import jax
import jax.numpy as jnp
from jax import lax
from jax.experimental import pallas as pl
from jax.experimental.pallas import tpu as pltpu

F32 = jnp.float32
BF16 = jnp.bfloat16
HIGHEST = lax.Precision.HIGHEST

V7X_LANES = 128
V7X_SUBLANES = 8
V7X_VMEM_LIMIT_BYTES = 56 * 1024 * 1024

EPS = 1e-6
GDN_HEADS = 8
GDN_DK = 128
GDN_CHUNK = 64
CONV_K = 4
DIFF_HEADS = 4
DIFF_DH = 128
ROT_DIM = DIFF_DH // 4
ROPE_THETA = 500000.0
LOG2E = 1.4426950408889634

TS_PROJ = 512
TS_SCAN = 512
TM_MLP = 512
TQ_ATTN = 512
FF_CHUNK = 1024


def _params(sem, vmem=V7X_VMEM_LIMIT_BYTES):
    return pltpu.CompilerParams(dimension_semantics=sem, vmem_limit_bytes=vmem)


def _sigmoid(x):
    return 1.0 / (1.0 + jnp.exp(-x))


def _silu(x):
    return x * _sigmoid(x)


def _softplus(x):
    return jnp.maximum(x, 0.0) + jnp.log1p(jnp.exp(-jnp.abs(x)))


def _dot(a, b, precision=None):
    return jnp.dot(a, b, preferred_element_type=F32, precision=precision)


def _dot_nt(a, b):
    return lax.dot_general(a, b, (((1,), (1,)), ((), ())), preferred_element_type=F32)


def _dot_tn(a, b):
    return lax.dot_general(a, b, (((0,), (0,)), ((), ())), preferred_element_type=F32)


def _mod_kernel(ct_ref, w_ref, b_ref, o_ref, *, batch):
    cs = _silu(ct_ref[...])
    w = w_ref[0]
    for b in range(batch):
        o_ref[0, b:b + 1, :] = jnp.sum(w * cs[:, b:b + 1], axis=0, keepdims=True) + b_ref[0]


def _mod_call(c_t, w, bias, tn=2048):
    n_l, d, n = w.shape
    batch = c_t.shape[1]
    return pl.pallas_call(
        functools.partial(_mod_kernel, batch=batch),
        grid=(n_l, n // tn),
        in_specs=[
            pl.BlockSpec((d, batch), lambda l, j: (0, 0)),
            pl.BlockSpec((1, d, tn), lambda l, j: (l, 0, j)),
            pl.BlockSpec((1, 1, tn), lambda l, j: (l, 0, j)),
        ],
        out_specs=pl.BlockSpec((1, batch, tn), lambda l, j: (l, 0, j)),
        out_shape=jax.ShapeDtypeStruct((n_l, batch, n), F32),
        compiler_params=_params(("arbitrary", "arbitrary")),
        name="mod_vectors",
    )(c_t, w, bias)


def _modulated_norm(x, g_row, shift_row, scale_row):
    rstd = lax.rsqrt(jnp.mean(x * x, axis=-1, keepdims=True) + EPS)
    return (x * rstd) * (g_row * (1.0 + scale_row)) + shift_row


def _gdn_inproj_kernel(x_ref, mod_ref, g_ref, w_ref, wab_ref, cw_ref, ad_ref,
                       o_ref, gb_ref, h_ref, big_ref, carry_ref, *, ts):
    s = pl.program_id(1)
    j = pl.program_id(2)

    @pl.when(j == 0)
    def _():
        m = mod_ref[0]
        h = _modulated_norm(x_ref[0], g_ref[...], m[0:1], m[1:2])
        hb = h.astype(BF16)
        h_ref[...] = hb
        ab = _dot(hb, wab_ref[...])
        ad = ad_ref[...]
        g = -jnp.exp(ad[0:1]) * _softplus(ab + ad[1:2])
        lane = lax.broadcasted_iota(jnp.int32, ab.shape, 1)
        gb_ref[0] = jnp.where(lane < GDN_HEADS, g, _sigmoid(ab))

    proj = _dot(h_ref[...], w_ref[...])

    @pl.when(j == 3)
    def _():
        o_ref[0] = proj

    @pl.when(j < 3)
    def _():
        jj = jnp.minimum(j, 2)
        prev = carry_ref[jj]
        big_ref[0:V7X_SUBLANES] = jnp.where(s == 0, jnp.zeros_like(prev), prev)
        big_ref[V7X_SUBLANES:] = proj
        carry_ref[jj] = proj[ts - V7X_SUBLANES:]
        cw = cw_ref[...]
        base = V7X_SUBLANES - (CONV_K - 1)
        y = big_ref[base:base + ts] * cw[0:1]
        for t in range(1, CONV_K):
            y = y + big_ref[base + t:base + t + ts] * cw[t:t + 1]
        y = _silu(y)

        @pl.when(j == 2)
        def _():
            o_ref[0] = y

        @pl.when(j < 2)
        def _():
            for hh in range(GDN_HEADS):
                cs = slice(hh * GDN_DK, (hh + 1) * GDN_DK)
                yh = y[:, cs]
                o_ref[0, :, cs] = yh * lax.rsqrt(jnp.sum(yh * yh, axis=-1, keepdims=True) + EPS)


def _gdn_inproj(x, mod, g_row, w, wab, cw, ad, ts=TS_PROJ):
    bsz, seq, d = x.shape
    n_col = w.shape[1] // d
    return pl.pallas_call(
        functools.partial(_gdn_inproj_kernel, ts=ts),
        grid=(bsz, seq // ts, n_col),
        in_specs=[
            pl.BlockSpec((1, ts, d), lambda b, s, j: (b, s, 0)),
            pl.BlockSpec((1, 6, d), lambda b, s, j: (b, 0, 0)),
            pl.BlockSpec((1, d), lambda b, s, j: (0, 0)),
            pl.BlockSpec((d, d), lambda b, s, j: (0, j)),
            pl.BlockSpec((d, V7X_LANES), lambda b, s, j: (0, 0)),
            pl.BlockSpec((CONV_K, d), lambda b, s, j: (0, jnp.minimum(j, 2))),
            pl.BlockSpec((2, V7X_LANES), lambda b, s, j: (0, 0)),
        ],
        out_specs=[
            pl.BlockSpec((1, ts, d), lambda b, s, j: (b, s, j)),
            pl.BlockSpec((1, ts, V7X_LANES), lambda b, s, j: (b, s, 0)),
        ],
        out_shape=[
            jax.ShapeDtypeStruct((bsz, seq, n_col * d), F32),
            jax.ShapeDtypeStruct((bsz, seq, V7X_LANES), F32),
        ],
        scratch_shapes=[
            pltpu.VMEM((ts, d), BF16),
            pltpu.VMEM((ts + V7X_SUBLANES, d), F32),
            pltpu.VMEM((3, V7X_SUBLANES, d), F32),
        ],
        compiler_params=_params(("arbitrary", "arbitrary", "arbitrary")),
        name="gdn_inproj",
    )(x, mod, g_row, w, wab, cw, ad)


def _gdn_scan_kernel(q_ref, k_ref, v_ref, z_ref, gb_ref, og_ref, y_ref, state_ref, *, ts):
    c_sz = GDN_CHUNK

    @pl.when(pl.program_id(1) == 0)
    def _():
        state_ref[...] = jnp.zeros_like(state_ref)

    row = lax.broadcasted_iota(jnp.int32, (c_sz, c_sz), 0)
    col = lax.broadcasted_iota(jnp.int32, (c_sz, c_sz), 1)
    incl = row >= col
    strict = row > col
    ltri = incl.astype(F32)
    eye = (row == col).astype(F32)
    og = og_ref[...]
    q_scale = GDN_DK ** -0.5

    def chunk(c, carry):
        r0 = pl.multiple_of(c * c_sz, c_sz)
        rows = pl.ds(r0, c_sz)
        gbc = gb_ref[0, rows, :]
        gcum = _dot(ltri, gbc, HIGHEST)
        gcum_t = gcum.T
        for h in range(GDN_HEADS):
            cs = slice(h * GDN_DK, (h + 1) * GDN_DK)
            qh = q_ref[0, rows, cs] * q_scale
            kh = k_ref[0, rows, cs]
            vh = v_ref[0, rows, cs]
            gc = gcum[:, h:h + 1]
            gr = gcum_t[h:h + 1, :]
            beta = gbc[:, GDN_HEADS + h:GDN_HEADS + h + 1]
            glast = gcum[c_sz - 1:c_sz, h:h + 1]
            eg = jnp.exp(gc)
            decay = jnp.where(incl, jnp.exp(jnp.where(incl, gc - gr, 0.0)), 0.0)
            kb = kh * beta
            kk = _dot_nt(jnp.concatenate([kb, qh], axis=0).astype(BF16), kh.astype(BF16))
            a_mat = jnp.where(strict, kk[:c_sz] * decay, 0.0)
            qk = kk[c_sz:] * decay
            x_mat = eye - a_mat
            p_mat = _dot(a_mat, a_mat, HIGHEST)
            for _ in range(4):
                xp = _dot(jnp.concatenate([x_mat, p_mat], axis=0), p_mat, HIGHEST)
                x_mat = x_mat + xp[:c_sz]
                p_mat = xp[c_sz:]
            x_mat = x_mat + _dot(x_mat, p_mat, HIGHEST)
            rhs = jnp.concatenate([vh * beta, kb * eg], axis=1)
            sol = _dot(x_mat, rhs, HIGHEST)
            u = sol[:, :GDN_DK]
            w = sol[:, GDN_DK:]
            st = state_ref[h]
            wq = _dot(jnp.concatenate([w, qh * eg], axis=0).astype(BF16), st.astype(BF16))
            v_new = u - wq[:c_sz]
            v_new_b = v_new.astype(BF16)
            o = wq[c_sz:] + _dot(qk.astype(BF16), v_new_b)
            kdec = kh * jnp.exp(glast - gc)
            state_ref[h] = st * jnp.exp(glast) + _dot_tn(kdec.astype(BF16), v_new_b)
            on = o * lax.rsqrt(jnp.mean(o * o, axis=-1, keepdims=True) + EPS) * og
            y_ref[0, rows, cs] = (on * _silu(z_ref[0, rows, cs])).astype(BF16)
        return carry

    lax.fori_loop(0, ts // c_sz, chunk, 0)


def _gdn_scan(qkvz, gb, og_row, ts=TS_SCAN):
    bsz, seq, _ = qkvz.shape
    d = GDN_HEADS * GDN_DK
    col = lambda j: pl.BlockSpec((1, ts, d), lambda b, s: (b, s, j))
    return pl.pallas_call(
        functools.partial(_gdn_scan_kernel, ts=ts),
        grid=(bsz, seq // ts),
        in_specs=[col(0), col(1), col(2), col(3),
                  pl.BlockSpec((1, ts, V7X_LANES), lambda b, s: (b, s, 0)),
                  pl.BlockSpec((1, GDN_DK), lambda b, s: (0, 0))],
        out_specs=pl.BlockSpec((1, ts, d), lambda b, s: (b, s, 0)),
        out_shape=jax.ShapeDtypeStruct((bsz, seq, d), BF16),
        scratch_shapes=[pltpu.VMEM((GDN_HEADS, GDN_DK, GDN_DK), F32)],
        compiler_params=_params(("arbitrary", "arbitrary")),
        name="gdn_scan",
    )(qkvz, qkvz, qkvz, qkvz, gb, og_row)


def _mixer_out_mlp_kernel(x_ref, y_ref, mod_ref, wo_ref, g_ref, w1_ref, w2_ref, fg_ref, o_ref,
                          *, final_norm):
    m = mod_ref[0]
    x1 = x_ref[0] + m[2:3] * _dot(y_ref[0], wo_ref[...])
    h = _modulated_norm(x1, g_ref[...], m[3:4], m[4:5]).astype(BF16)
    d_ff = w1_ref.shape[1]
    acc = None
    for c in range(d_ff // FF_CHUNK):
        cs = slice(c * FF_CHUNK, (c + 1) * FF_CHUNK)
        hid = jnp.maximum(_dot(h, w1_ref[:, cs]), 0.0)
        part = _dot((hid * hid).astype(BF16), w2_ref[cs, :])
        acc = part if acc is None else acc + part
    x2 = x1 + m[5:6] * acc
    if final_norm:
        x2 = x2 * lax.rsqrt(jnp.mean(x2 * x2, axis=-1, keepdims=True) + EPS) * fg_ref[...]
    o_ref[0] = x2


def _mixer_out_mlp(x, y, mod, wo, g_row, w1, w2, fg_row, final_norm, tm=TM_MLP):
    bsz, seq, d = x.shape
    d_ff = w1.shape[1]
    const = lambda shape: pl.BlockSpec(shape, lambda b, s: (0,) * len(shape),
                                       pipeline_mode=pl.Buffered(1))
    return pl.pallas_call(
        functools.partial(_mixer_out_mlp_kernel, final_norm=final_norm),
        grid=(bsz, seq // tm),
        in_specs=[
            pl.BlockSpec((1, tm, d), lambda b, s: (b, s, 0)),
            pl.BlockSpec((1, tm, d), lambda b, s: (b, s, 0)),
            pl.BlockSpec((1, 6, d), lambda b, s: (b, 0, 0)),
            const((d, d)),
            const((1, d)),
            const((d, d_ff)),
            const((d_ff, d)),
            const((1, d)),
        ],
        out_specs=pl.BlockSpec((1, tm, d), lambda b, s: (b, s, 0)),
        out_shape=jax.ShapeDtypeStruct((bsz, seq, d), F32),
        compiler_params=_params(("arbitrary", "arbitrary")),
        name="mixer_out_mlp",
    )(x, y, mod, wo, g_row, w1, w2, fg_row)


def _rope(y, cos_t, sin_t):
    half = ROT_DIM // 2
    lane = lax.broadcasted_iota(jnp.int32, cos_t.shape, 1)
    outs = []
    for gidx in range(y.shape[1] // DIFF_DH):
        yh = y[:, gidx * DIFF_DH:(gidx + 1) * DIFF_DH]
        rot = jnp.where(lane < half, pltpu.roll(yh, DIFF_DH - half, 1), pltpu.roll(yh, half, 1))
        outs.append(yh * cos_t + rot * sin_t)
    return outs


def _attn_inproj_kernel(x_ref, mod_ref, kvmod_ref, gq_ref, gkv_ref, w_ref, cos_ref, sin_ref,
                        o_ref, hq_ref, hk_ref):
    j = pl.program_id(2)

    @pl.when(j == 0)
    def _():
        x = x_ref[0]
        xn = x * lax.rsqrt(jnp.mean(x * x, axis=-1, keepdims=True) + EPS)
        m = mod_ref[0]
        km = kvmod_ref[0]
        hq_ref[...] = (xn * (gq_ref[...] * (1.0 + m[1:2])) + m[0:1]).astype(BF16)
        hk_ref[...] = (xn * (gkv_ref[...] * (1.0 + km[1:2])) + km[0:1]).astype(BF16)

    @pl.when(j == 0)
    def _():
        y = _dot(hq_ref[...], w_ref[...])
        q_scale = (DIFF_DH ** -0.5) * LOG2E
        for gidx, yr in enumerate(_rope(y, cos_ref[...], sin_ref[...])):
            o_ref[0, :, gidx * DIFF_DH:(gidx + 1) * DIFF_DH] = (yr * q_scale).astype(BF16)

    @pl.when(j == 1)
    def _():
        y = _dot(hk_ref[...], w_ref[...])
        for gidx, yr in enumerate(_rope(y, cos_ref[...], sin_ref[...])):
            o_ref[0, :, gidx * DIFF_DH:(gidx + 1) * DIFF_DH] = yr.astype(BF16)

    @pl.when(j == 2)
    def _():
        o_ref[0] = _dot(hk_ref[...], w_ref[...]).astype(BF16)


def _attn_inproj(x, mod, kvmod, gq_row, gkv_row, w, cos_t, sin_t, ts=TS_PROJ):
    bsz, seq, d = x.shape
    n_col = w.shape[1] // d
    return pl.pallas_call(
        _attn_inproj_kernel,
        grid=(bsz, seq // ts, n_col),
        in_specs=[
            pl.BlockSpec((1, ts, d), lambda b, s, j: (b, s, 0)),
            pl.BlockSpec((1, 6, d), lambda b, s, j: (b, 0, 0)),
            pl.BlockSpec((1, 2, d), lambda b, s, j: (b, 0, 0)),
            pl.BlockSpec((1, d), lambda b, s, j: (0, 0)),
            pl.BlockSpec((1, d), lambda b, s, j: (0, 0)),
            pl.BlockSpec((d, d), lambda b, s, j: (0, j)),
            pl.BlockSpec((ts, DIFF_DH), lambda b, s, j: (s, 0)),
            pl.BlockSpec((ts, DIFF_DH), lambda b, s, j: (s, 0)),
        ],
        out_specs=pl.BlockSpec((1, ts, d), lambda b, s, j: (b, s, j)),
        out_shape=jax.ShapeDtypeStruct((bsz, seq, n_col * d), BF16),
        scratch_shapes=[pltpu.VMEM((ts, d), BF16), pltpu.VMEM((ts, d), BF16)],
        compiler_params=_params(("arbitrary", "arbitrary", "arbitrary")),
        name="attn_inproj",
    )(x, mod, kvmod, gq_row, gkv_row, w, cos_t, sin_t)


def _diff_attn_kernel(q_ref, k_ref, v_ref, lam_ref, g_ref, o_ref, m_ref, l_ref, acc_ref,
                      *, tq, lam_init):
    qi = pl.program_id(2)
    m_ref[...] = jnp.full(m_ref.shape, -jnp.inf, F32)
    l_ref[...] = jnp.zeros_like(l_ref)
    acc_ref[...] = jnp.zeros_like(acc_ref)

    def step(kstart, masked):
        rows = pl.ds(kstart, tq)
        v = v_ref[0, rows, :]
        for mi in range(2):
            cs = slice(mi * DIFF_DH, (mi + 1) * DIFF_DH)
            s = _dot_nt(q_ref[0, :, cs], k_ref[0, rows, cs])
            if masked:
                r = lax.broadcasted_iota(jnp.int32, s.shape, 0)
                c = lax.broadcasted_iota(jnp.int32, s.shape, 1)
                s = jnp.where(c <= r, s, -jnp.inf)
            m_old = m_ref[mi]
            m_new = jnp.maximum(m_old, jnp.max(s, axis=-1, keepdims=True))
            alpha = jnp.exp2(m_old - m_new)
            p = jnp.exp2(s - m_new)
            l_ref[mi] = alpha * l_ref[mi] + jnp.sum(p, axis=-1, keepdims=True)
            acc_ref[mi] = alpha * acc_ref[mi] + _dot(p.astype(BF16), v)
            m_ref[mi] = m_new

    def body(i, carry):
        step(pl.multiple_of(i * tq, tq), False)
        return carry

    lax.fori_loop(0, qi, body, 0)
    step(pl.multiple_of(qi * tq, tq), True)

    lp = lam_ref[...]
    lam = (jnp.exp(jnp.sum(lp[0:1] * lp[1:2], axis=-1, keepdims=True))
           - jnp.exp(jnp.sum(lp[2:3] * lp[3:4], axis=-1, keepdims=True)) + lam_init)
    o = acc_ref[0] / l_ref[0] - lam * (acc_ref[1] / l_ref[1])
    on = o * lax.rsqrt(jnp.mean(o * o, axis=-1, keepdims=True) + EPS) * g_ref[...]
    o_ref[0] = (on * (1.0 - lam_init)).astype(BF16)


def _diff_attn(qkv, lam_params, g_row, lam_init, tq=TQ_ATTN):
    bsz, seq, _ = qkv.shape
    hw = 2 * DIFF_DH
    return pl.pallas_call(
        functools.partial(_diff_attn_kernel, tq=tq, lam_init=lam_init),
        grid=(bsz, DIFF_HEADS, seq // tq),
        in_specs=[
            pl.BlockSpec((1, tq, hw), lambda b, h, i: (b, i, h)),
            pl.BlockSpec((1, seq, hw), lambda b, h, i: (b, 0, DIFF_HEADS + h)),
            pl.BlockSpec((1, seq, hw), lambda b, h, i: (b, 0, 2 * DIFF_HEADS + h)),
            pl.BlockSpec((4, DIFF_DH), lambda b, h, i: (0, 0)),
            pl.BlockSpec((1, hw), lambda b, h, i: (0, 0)),
        ],
        out_specs=pl.BlockSpec((1, tq, hw), lambda b, h, i: (b, i, h)),
        out_shape=jax.ShapeDtypeStruct((bsz, seq, DIFF_HEADS * hw), BF16),
        scratch_shapes=[
            pltpu.VMEM((2, tq, 1), F32),
            pltpu.VMEM((2, tq, 1), F32),
            pltpu.VMEM((2, tq, hw), F32),
        ],
        compiler_params=_params(("arbitrary", "arbitrary", "arbitrary")),
        name="diff_attn",
    )(qkv, qkv, qkv, lam_params, g_row)


def _rope_tables(seq):
    pos = jnp.arange(seq, dtype=F32)
    inv_freq = ROPE_THETA ** (-jnp.arange(0, ROT_DIM, 2, dtype=F32) / ROT_DIM)
    freqs = pos[:, None] * inv_freq[None, :]
    cos, sin = jnp.cos(freqs), jnp.sin(freqs)
    pad = DIFF_DH - ROT_DIM
    cos_t = jnp.concatenate([cos, cos, jnp.ones((seq, pad), F32)], axis=1)
    sin_t = jnp.concatenate([-sin, sin, jnp.zeros((seq, pad), F32)], axis=1)
    return cos_t, sin_t


def kernel(x, c, mod_w, mod_b, norm_mix_g, norm_mlp_g, a_w_in, a_conv_w, a_log, a_dt_bias, a_out_norm_g, a_w_out,
           kv_mod_w, kv_mod_b, kv_norm_g, kv_w, b_w_q, b_lambda, b_subln_g, b_w_out, mlp_w1, mlp_w2, final_g):
    bsz, seq, d = x.shape
    depth = mod_w.shape[0]
    assert depth == 2 and a_w_in.shape[0] == 1 and b_w_q.shape[0] == 1
    assert seq % TS_PROJ == 0 and seq % TS_SCAN == 0 and seq % TM_MLP == 0 and seq % TQ_ATTN == 0

    c_t = c.T
    mod = _mod_call(c_t, mod_w, mod_b[:, None, :]).reshape(depth, bsz, 6, d)
    kvmod = _mod_call(c_t, kv_mod_w[None], kv_mod_b[None, None, :]).reshape(bsz, 2, d)

    row = lambda v: v.reshape(1, -1).astype(F32)

    w_in = a_w_in[0]
    n_main = 4 * d
    w_main = w_in[:, :n_main].astype(BF16)
    w_ab = jnp.pad(w_in[:, n_main:], ((0, 0), (0, V7X_LANES - 2 * GDN_HEADS))).astype(BF16)
    ad = jnp.pad(jnp.stack([a_log[0], a_dt_bias[0]]).astype(F32), ((0, 0), (0, V7X_LANES - GDN_HEADS)))
    qkvz, gb = _gdn_inproj(x, mod[0], row(norm_mix_g[0]), w_main, w_ab, a_conv_w[0].astype(F32), ad)
    y0 = _gdn_scan(qkvz, gb, row(a_out_norm_g[0]))
    x = _mixer_out_mlp(x, y0, mod[0], a_w_out[0].astype(BF16), row(norm_mlp_g[0]),
                       mlp_w1[0].astype(BF16), mlp_w2[0].astype(BF16), row(final_g), False)

    lam_init = 0.8 - 0.6 * math.exp(-0.3 * 1)
    w_qkv = jnp.concatenate([b_w_q[0], kv_w], axis=1).astype(BF16)
    cos_t, sin_t = _rope_tables(seq)
    qkv = _attn_inproj(x, mod[1], kvmod, row(norm_mix_g[1]), row(kv_norm_g), w_qkv, cos_t, sin_t)
    y1 = _diff_attn(qkv, b_lambda[0].astype(F32), row(b_subln_g[0]), lam_init)
    x = _mixer_out_mlp(x, y1, mod[1], b_w_out[0].astype(BF16), row(norm_mlp_g[1]),
                       mlp_w1[1].astype(BF16), mlp_w2[1].astype(BF16), row(final_g), True)
    return x
```

```python
import functools
import math

import jax
import jax.numpy as jnp
from jax import lax
from jax.experimental import pallas as pl
from jax.experimental.pallas import tpu as pltpu

F32 = jnp.float32
BF16 = jnp.bfloat16
HIGHEST = lax.Precision.HIGHEST

V7X_LANES = 128
V7X_SUBLANES = 8
V7X_VMEM_LIMIT_BYTES = 56 * 1024 * 1024

EPS = 1e-6
GDN_HEADS = 8
GDN_DK = 128
GDN_CHUNK = 64
CONV_K = 4
DIFF_HEADS = 4
DIFF_DH = 128
ROT_DIM = DIFF_DH // 4
ROPE_THETA = 500000.0
LOG2E = 1.4426950408889634

TS_PROJ = 512
TS_SCAN = 512
TM_MLP = 512
TQ_ATTN = 1024
ATTN_ROW_TILE = 128
FF_CHUNK = 1024


def _params(sem, vmem=V7X_VMEM_LIMIT_BYTES, flags=None):
    return pltpu.CompilerParams(dimension_semantics=sem, vmem_limit_bytes=vmem, flags=flags)


def _sigmoid(x):
    return 1.0 / (1.0 + jnp.exp(-x))


def _silu(x):
    return x * _sigmoid(x)


def _softplus(x):
    return jnp.maximum(x, 0.0) + jnp.log1p(jnp.exp(-jnp.abs(x)))


def _dot(a, b, precision=None):
    return jnp.dot(a, b, preferred_element_type=F32, precision=precision)


def _dot_nt(a, b):
    return lax.dot_general(a, b, (((1,), (1,)), ((), ())), preferred_element_type=F32)


def _dot_tn(a, b):
    return lax.dot_general(a, b, (((0,), (0,)), ((), ())), preferred_element_type=F32)


def _mod_kernel(ct_ref, w_ref, b_ref, o_ref, *, batch):
    cs = _silu(ct_ref[...])
    w = w_ref[0]
    for b in range(batch):
        o_ref[0, b:b + 1, :] = jnp.sum(w * cs[:, b:b + 1], axis=0, keepdims=True) + b_ref[0]


def _mod_call(c_t, w, bias, tn=2048):
    n_l, d, n = w.shape
    batch = c_t.shape[1]
    return pl.pallas_call(
        functools.partial(_mod_kernel, batch=batch),
        grid=(n_l, n // tn),
        in_specs=[
            pl.BlockSpec((d, batch), lambda l, j: (0, 0)),
            pl.BlockSpec((1, d, tn), lambda l, j: (l, 0, j)),
            pl.BlockSpec((1, 1, tn), lambda l, j: (l, 0, j)),
        ],
        out_specs=pl.BlockSpec((1, batch, tn), lambda l, j: (l, 0, j)),
        out_shape=jax.ShapeDtypeStruct((n_l, batch, n), F32),
        compiler_params=_params(("arbitrary", "arbitrary")),
        name="mod_vectors",
    )(c_t, w, bias)


def _modulated_norm(x, g_row, shift_row, scale_row):
    rstd = lax.rsqrt(jnp.mean(x * x, axis=-1, keepdims=True) + EPS)
    return (x * rstd) * (g_row * (1.0 + scale_row)) + shift_row


def _gdn_inproj_kernel(x_ref, mod_ref, g_ref, w_ref, wab_ref, cw_ref, ad_ref,
                       o_ref, gb_ref, h_ref, big_ref, carry_ref, *, ts):
    s = pl.program_id(1)
    j = pl.program_id(2)

    @pl.when(j == 0)
    def _():
        m = mod_ref[0]
        h = _modulated_norm(x_ref[0], g_ref[...], m[0:1], m[1:2])
        hb = h.astype(BF16)
        h_ref[...] = hb
        ab = _dot(hb, wab_ref[...])
        ad = ad_ref[...]
        g = -jnp.exp(ad[0:1]) * _softplus(ab + ad[1:2])
        lane = lax.broadcasted_iota(jnp.int32, ab.shape, 1)
        gb_ref[0] = jnp.where(lane < GDN_HEADS, g, _sigmoid(ab))

    proj = _dot(h_ref[...], w_ref[...])

    @pl.when(j == 3)
    def _():
        o_ref[0] = proj

    @pl.when(j < 3)
    def _():
        jj = jnp.minimum(j, 2)
        prev = carry_ref[jj]
        big_ref[0:V7X_SUBLANES] = jnp.where(s == 0, jnp.zeros_like(prev), prev)
        big_ref[V7X_SUBLANES:] = proj
        carry_ref[jj] = proj[ts - V7X_SUBLANES:]
        cw = cw_ref[...]
        base = V7X_SUBLANES - (CONV_K - 1)
        y = big_ref[base:base + ts] * cw[0:1]
        for t in range(1, CONV_K):
            y = y + big_ref[base + t:base + t + ts] * cw[t:t + 1]
        y = _silu(y)

        @pl.when(j == 2)
        def _():
            o_ref[0] = y

        @pl.when(j < 2)
        def _():
            for hh in range(GDN_HEADS):
                cs = slice(hh * GDN_DK, (hh + 1) * GDN_DK)
                yh = y[:, cs]
                o_ref[0, :, cs] = yh * lax.rsqrt(jnp.sum(yh * yh, axis=-1, keepdims=True) + EPS)


def _gdn_inproj(x, mod, g_row, w, wab, cw, ad, ts=TS_PROJ):
    bsz, seq, d = x.shape
    n_col = w.shape[1] // d
    return pl.pallas_call(
        functools.partial(_gdn_inproj_kernel, ts=ts),
        grid=(bsz, seq // ts, n_col),
        in_specs=[
            pl.BlockSpec((1, ts, d), lambda b, s, j: (b, s, 0)),
            pl.BlockSpec((1, 6, d), lambda b, s, j: (b, 0, 0)),
            pl.BlockSpec((1, d), lambda b, s, j: (0, 0)),
            pl.BlockSpec((d, d), lambda b, s, j: (0, j)),
            pl.BlockSpec((d, V7X_LANES), lambda b, s, j: (0, 0)),
            pl.BlockSpec((CONV_K, d), lambda b, s, j: (0, jnp.minimum(j, 2))),
            pl.BlockSpec((2, V7X_LANES), lambda b, s, j: (0, 0)),
        ],
        out_specs=[
            pl.BlockSpec((1, ts, d), lambda b, s, j: (b, s, j)),
            pl.BlockSpec((1, ts, V7X_LANES), lambda b, s, j: (b, s, 0)),
        ],
        out_shape=[
            jax.ShapeDtypeStruct((bsz, seq, n_col * d), F32),
            jax.ShapeDtypeStruct((bsz, seq, V7X_LANES), F32),
        ],
        scratch_shapes=[
            pltpu.VMEM((ts, d), BF16),
            pltpu.VMEM((ts + V7X_SUBLANES, d), F32),
            pltpu.VMEM((3, V7X_SUBLANES, d), F32),
        ],
        compiler_params=_params(("arbitrary", "arbitrary", "arbitrary")),
        name="gdn_inproj",
    )(x, mod, g_row, w, wab, cw, ad)


def _gdn_scan_kernel(q_ref, k_ref, v_ref, z_ref, gb_ref, og_ref, y_ref, state_ref, *, ts):
    c_sz = GDN_CHUNK

    @pl.when(pl.program_id(1) == 0)
    def _():
        state_ref[...] = jnp.zeros_like(state_ref)

    row = lax.broadcasted_iota(jnp.int32, (c_sz, c_sz), 0)
    col = lax.broadcasted_iota(jnp.int32, (c_sz, c_sz), 1)
    incl = row >= col
    strict = row > col
    ltri = incl.astype(F32)
    eye = (row == col).astype(F32)
    og = og_ref[...]
    q_scale = GDN_DK ** -0.5

    def chunk(c, carry):
        r0 = pl.multiple_of(c * c_sz, c_sz)
        rows = pl.ds(r0, c_sz)
        gbc = gb_ref[0, rows, :]
        gcum = _dot(ltri, gbc, HIGHEST)
        gcum_t = gcum.T
        heads = range(GDN_HEADS)
        cols = [slice(h * GDN_DK, (h + 1) * GDN_DK) for h in heads]
        qh = [q_ref[0, rows, cols[h]] * q_scale for h in heads]
        kh = [k_ref[0, rows, cols[h]] for h in heads]
        gc = [gcum[:, h:h + 1] for h in heads]
        beta = [gbc[:, GDN_HEADS + h:GDN_HEADS + h + 1] for h in heads]
        glast = [gcum[c_sz - 1:c_sz, h:h + 1] for h in heads]
        eg = [jnp.exp(gc[h]) for h in heads]
        decay = [jnp.where(incl, jnp.exp(jnp.where(incl, gc[h] - gcum_t[h:h + 1, :], 0.0)), 0.0)
                 for h in heads]
        kb = [kh[h] * beta[h] for h in heads]
        kk = [_dot_nt(jnp.concatenate([kb[h], qh[h]], axis=0).astype(BF16), kh[h].astype(BF16))
              for h in heads]
        a_mat = [jnp.where(strict, kk[h][:c_sz] * decay[h], 0.0) for h in heads]
        qk = [(kk[h][c_sz:] * decay[h]).astype(BF16) for h in heads]
        x_mat = [eye - a_mat[h] for h in heads]
        p_mat = [_dot(a_mat[h], a_mat[h], HIGHEST) for h in heads]
        for _ in range(4):
            xp = [_dot(jnp.concatenate([x_mat[h], p_mat[h]], axis=0), p_mat[h], HIGHEST) for h in heads]
            x_mat = [x_mat[h] + xp[h][:c_sz] for h in heads]
            p_mat = [xp[h][c_sz:] for h in heads]
        xl = [_dot(x_mat[h], p_mat[h], HIGHEST) for h in heads]
        x_mat = [x_mat[h] + xl[h] for h in heads]
        sol = [_dot(x_mat[h],
                    jnp.concatenate([v_ref[0, rows, cols[h]] * beta[h], kb[h] * eg[h]], axis=1),
                    HIGHEST) for h in heads]
        st = [state_ref[h] for h in heads]
        wq = [_dot(jnp.concatenate([sol[h][:, GDN_DK:], qh[h] * eg[h]], axis=0).astype(BF16),
                   st[h].astype(BF16)) for h in heads]
        v_new = [(sol[h][:, :GDN_DK] - wq[h][:c_sz]).astype(BF16) for h in heads]
        o = [wq[h][c_sz:] + _dot(qk[h], v_new[h]) for h in heads]
        for h in heads:
            kdec = kh[h] * jnp.exp(glast[h] - gc[h])
            state_ref[h] = st[h] * jnp.exp(glast[h]) + _dot_tn(kdec.astype(BF16), v_new[h])
        for h in heads:
            on = o[h] * lax.rsqrt(jnp.mean(o[h] * o[h], axis=-1, keepdims=True) + EPS) * og
            y_ref[0, rows, cols[h]] = (on * _silu(z_ref[0, rows, cols[h]])).astype(BF16)
        return carry

    lax.fori_loop(0, ts // c_sz, chunk, 0)


def _gdn_scan(qkvz, gb, og_row, ts=TS_SCAN):
    bsz, seq, _ = qkvz.shape
    d = GDN_HEADS * GDN_DK
    col = lambda j: pl.BlockSpec((1, ts, d), lambda b, s: (b, s, j))
    return pl.pallas_call(
        functools.partial(_gdn_scan_kernel, ts=ts),
        grid=(bsz, seq // ts),
        in_specs=[col(0), col(1), col(2), col(3),
                  pl.BlockSpec((1, ts, V7X_LANES), lambda b, s: (b, s, 0)),
                  pl.BlockSpec((1, GDN_DK), lambda b, s: (0, 0))],
        out_specs=pl.BlockSpec((1, ts, d), lambda b, s: (b, s, 0)),
        out_shape=jax.ShapeDtypeStruct((bsz, seq, d), BF16),
        scratch_shapes=[pltpu.VMEM((GDN_HEADS, GDN_DK, GDN_DK), F32)],
        compiler_params=_params(("arbitrary", "arbitrary")),
        name="gdn_scan",
    )(qkvz, qkvz, qkvz, qkvz, gb, og_row)


def _mixer_out_mlp_kernel(x_ref, y_ref, mod_ref, wo_ref, g_ref, w1_ref, w2_ref, fg_ref, o_ref,
                          *, final_norm):
    m = mod_ref[0]
    x1 = x_ref[0] + m[2:3] * _dot(y_ref[0], wo_ref[...])
    h = _modulated_norm(x1, g_ref[...], m[3:4], m[4:5]).astype(BF16)
    d_ff = w1_ref.shape[1]
    acc = None
    for c in range(d_ff // FF_CHUNK):
        cs = slice(c * FF_CHUNK, (c + 1) * FF_CHUNK)
        hid = jnp.maximum(_dot(h, w1_ref[:, cs]), 0.0)
        part = _dot((hid * hid).astype(BF16), w2_ref[cs, :])
        acc = part if acc is None else acc + part
    x2 = x1 + m[5:6] * acc
    if final_norm:
        x2 = x2 * lax.rsqrt(jnp.mean(x2 * x2, axis=-1, keepdims=True) + EPS) * fg_ref[...]
    o_ref[0] = x2


def _mixer_out_mlp(x, y, mod, wo, g_row, w1, w2, fg_row, final_norm, tm=TM_MLP):
    bsz, seq, d = x.shape
    d_ff = w1.shape[1]
    const = lambda shape: pl.BlockSpec(shape, lambda b, s: (0,) * len(shape),
                                       pipeline_mode=pl.Buffered(1))
    return pl.pallas_call(
        functools.partial(_mixer_out_mlp_kernel, final_norm=final_norm),
        grid=(bsz, seq // tm),
        in_specs=[
            pl.BlockSpec((1, tm, d), lambda b, s: (b, s, 0)),
            pl.BlockSpec((1, tm, d), lambda b, s: (b, s, 0)),
            pl.BlockSpec((1, 6, d), lambda b, s: (b, 0, 0)),
            const((d, d)),
            const((1, d)),
            const((d, d_ff)),
            const((d_ff, d)),
            const((1, d)),
        ],
        out_specs=pl.BlockSpec((1, tm, d), lambda b, s: (b, s, 0)),
        out_shape=jax.ShapeDtypeStruct((bsz, seq, d), F32),
        compiler_params=_params(("arbitrary", "arbitrary")),
        name="mixer_out_mlp",
    )(x, y, mod, wo, g_row, w1, w2, fg_row)


def _rope(y, cos_t, sin_t):
    half = ROT_DIM // 2
    lane = lax.broadcasted_iota(jnp.int32, cos_t.shape, 1)
    outs = []
    for gidx in range(y.shape[1] // DIFF_DH):
        yh = y[:, gidx * DIFF_DH:(gidx + 1) * DIFF_DH]
        rot = jnp.where(lane < half, pltpu.roll(yh, DIFF_DH - half, 1), pltpu.roll(yh, half, 1))
        outs.append(yh * cos_t + rot * sin_t)
    return outs


def _attn_inproj_kernel(x_ref, mod_ref, kvmod_ref, gq_ref, gkv_ref, w_ref, cos_ref, sin_ref,
                        o_ref, hq_ref, hk_ref):
    j = pl.program_id(2)

    @pl.when(j == 0)
    def _():
        x = x_ref[0]
        xn = x * lax.rsqrt(jnp.mean(x * x, axis=-1, keepdims=True) + EPS)
        m = mod_ref[0]
        km = kvmod_ref[0]
        hq_ref[...] = (xn * (gq_ref[...] * (1.0 + m[1:2])) + m[0:1]).astype(BF16)
        hk_ref[...] = (xn * (gkv_ref[...] * (1.0 + km[1:2])) + km[0:1]).astype(BF16)

    @pl.when(j == 0)
    def _():
        y = _dot(hq_ref[...], w_ref[...])
        q_scale = (DIFF_DH ** -0.5) * LOG2E
        for gidx, yr in enumerate(_rope(y, cos_ref[...], sin_ref[...])):
            o_ref[0, :, gidx * DIFF_DH:(gidx + 1) * DIFF_DH] = (yr * q_scale).astype(BF16)

    @pl.when(j == 1)
    def _():
        y = _dot(hk_ref[...], w_ref[...])
        for gidx, yr in enumerate(_rope(y, cos_ref[...], sin_ref[...])):
            o_ref[0, :, gidx * DIFF_DH:(gidx + 1) * DIFF_DH] = yr.astype(BF16)

    @pl.when(j == 2)
    def _():
        o_ref[0] = _dot(hk_ref[...], w_ref[...]).astype(BF16)


def _attn_inproj(x, mod, kvmod, gq_row, gkv_row, w, cos_t, sin_t, ts=TS_PROJ):
    bsz, seq, d = x.shape
    n_col = w.shape[1] // d
    return pl.pallas_call(
        _attn_inproj_kernel,
        grid=(bsz, seq // ts, n_col),
        in_specs=[
            pl.BlockSpec((1, ts, d), lambda b, s, j: (b, s, 0)),
            pl.BlockSpec((1, 6, d), lambda b, s, j: (b, 0, 0)),
            pl.BlockSpec((1, 2, d), lambda b, s, j: (b, 0, 0)),
            pl.BlockSpec((1, d), lambda b, s, j: (0, 0)),
            pl.BlockSpec((1, d), lambda b, s, j: (0, 0)),
            pl.BlockSpec((d, d), lambda b, s, j: (0, j)),
            pl.BlockSpec((ts, DIFF_DH), lambda b, s, j: (s, 0)),
            pl.BlockSpec((ts, DIFF_DH), lambda b, s, j: (s, 0)),
        ],
        out_specs=pl.BlockSpec((1, ts, d), lambda b, s, j: (b, s, j)),
        out_shape=jax.ShapeDtypeStruct((bsz, seq, n_col * d), BF16),
        scratch_shapes=[pltpu.VMEM((ts, d), BF16), pltpu.VMEM((ts, d), BF16)],
        compiler_params=_params(("arbitrary", "arbitrary", "arbitrary")),
        name="attn_inproj",
    )(x, mod, kvmod, gq_row, gkv_row, w, cos_t, sin_t)


def _diff_attn_kernel(q_ref, k_ref, v_ref, lam_ref, g_ref, o_ref, m_ref, l_ref, acc_ref,
                      sa_ref, sb_ref, mxa_ref, mxb_ref, al_ref, p_ref, *, tq, tk, lam_init):
    qi = pl.program_id(2)
    m_ref[...] = jnp.full(m_ref.shape, -jnp.inf, F32)
    l_ref[...] = jnp.zeros_like(l_ref)
    acc_ref[...] = jnp.zeros_like(acc_ref)

    lane_chunks = [slice(j * V7X_LANES, (j + 1) * V7X_LANES) for j in range(tk // V7X_LANES)]
    row_tiles = [slice(r, r + ATTN_ROW_TILE) for r in range(0, tq, ATTN_ROW_TILE)]

    def produce(kblk, s_ref, mx_ref):
        rows = pl.ds(pl.multiple_of(kblk * tk, tk), tk)
        for mi in range(2):
            cs = slice(mi * DIFF_DH, (mi + 1) * DIFF_DH)
            s = _dot_nt(q_ref[0, :, cs], k_ref[0, rows, cs])
            s_ref[mi] = s
            mx_ref[mi] = functools.reduce(jnp.maximum, [s[:, c] for c in lane_chunks])

    def consume(kblk, s_ref, mx_ref, masked):
        kstart = pl.multiple_of(kblk * tk, tk)
        v = v_ref[0, pl.ds(kstart, tk), :]
        for mi in range(2):
            if masked:
                off = kstart - qi * tq
                for rt in row_tiles:
                    r = lax.broadcasted_iota(jnp.int32, (ATTN_ROW_TILE, V7X_LANES), 0) + rt.start
                    c = lax.broadcasted_iota(jnp.int32, (ATTN_ROW_TILE, V7X_LANES), 1)
                    diff = r - c - off
                    sc = [jnp.where(diff >= c_.start, s_ref[mi, rt, c_], -jnp.inf) for c_ in lane_chunks]
                    for c_, x in zip(lane_chunks, sc):
                        s_ref[mi, rt, c_] = x
                    mx_ref[mi, rt] = functools.reduce(jnp.maximum, sc)
            m_old = m_ref[mi]
            m_new = jnp.maximum(m_old, jnp.max(mx_ref[mi], axis=-1, keepdims=True))
            al_ref[mi] = jnp.exp2(m_old - m_new)
            m_ref[mi] = m_new
            for rt in row_tiles:
                m_r = m_ref[mi, rt]
                ps = [jnp.exp2(s_ref[mi, rt, c_] - m_r) for c_ in lane_chunks]
                l_ref[mi, rt] = al_ref[mi, rt] * l_ref[mi, rt] + functools.reduce(jnp.add, ps)
                p_ref[mi, rt] = jnp.concatenate(ps, axis=1).astype(BF16)
            pv = _dot(p_ref[mi], v)
            alpha = al_ref[mi]
            acc_ref[mi] = jnp.concatenate([alpha, alpha], axis=1) * acc_ref[mi] + pv

    produce(0, sa_ref, mxa_ref)

    def body(j, carry):
        produce(2 * j + 1, sb_ref, mxb_ref)
        consume(2 * j, sa_ref, mxa_ref, False)
        produce(2 * j + 2, sa_ref, mxa_ref)
        consume(2 * j + 1, sb_ref, mxb_ref, False)
        return carry

    lax.fori_loop(0, qi, body, 0)
    produce(2 * qi + 1, sb_ref, mxb_ref)
    consume(2 * qi, sa_ref, mxa_ref, True)
    consume(2 * qi + 1, sb_ref, mxb_ref, True)

    lp = lam_ref[...]
    lam = (jnp.exp(jnp.sum(lp[0:1] * lp[1:2], axis=-1, keepdims=True))
           - jnp.exp(jnp.sum(lp[2:3] * lp[3:4], axis=-1, keepdims=True)) + lam_init)
    l0 = jnp.sum(l_ref[0], axis=-1, keepdims=True)
    l1 = jnp.sum(l_ref[1], axis=-1, keepdims=True)
    o = acc_ref[0] / l0 - lam * (acc_ref[1] / l1)
    on = o * lax.rsqrt(jnp.mean(o * o, axis=-1, keepdims=True) + EPS) * g_ref[...]
    o_ref[0] = (on * (1.0 - lam_init)).astype(BF16)


def _diff_attn(qkv, lam_params, g_row, lam_init, tq=TQ_ATTN):
    bsz, seq, _ = qkv.shape
    hw = 2 * DIFF_DH
    tk = tq // 2
    return pl.pallas_call(
        functools.partial(_diff_attn_kernel, tq=tq, tk=tk, lam_init=lam_init),
        grid=(bsz, DIFF_HEADS, seq // tq),
        in_specs=[
            pl.BlockSpec((1, tq, hw), lambda b, h, i: (b, i, h)),
            pl.BlockSpec((1, seq, hw), lambda b, h, i: (b, 0, DIFF_HEADS + h),
                         pipeline_mode=pl.Buffered(1)),
            pl.BlockSpec((1, seq, hw), lambda b, h, i: (b, 0, 2 * DIFF_HEADS + h),
                         pipeline_mode=pl.Buffered(1)),
            pl.BlockSpec((4, DIFF_DH), lambda b, h, i: (0, 0)),
            pl.BlockSpec((1, hw), lambda b, h, i: (0, 0)),
        ],
        out_specs=pl.BlockSpec((1, tq, hw), lambda b, h, i: (b, i, h)),
        out_shape=jax.ShapeDtypeStruct((bsz, seq, DIFF_HEADS * hw), BF16),
        scratch_shapes=[
            pltpu.VMEM((2, tq, V7X_LANES), F32),
            pltpu.VMEM((2, tq, V7X_LANES), F32),
            pltpu.VMEM((2, tq, hw), F32),
            pltpu.VMEM((2, tq, tk), F32),
            pltpu.VMEM((2, tq, tk), F32),
            pltpu.VMEM((2, tq, V7X_LANES), F32),
            pltpu.VMEM((2, tq, V7X_LANES), F32),
            pltpu.VMEM((2, tq, V7X_LANES), F32),
            pltpu.VMEM((2, tq, tk), BF16),
        ],
        compiler_params=_params(("arbitrary", "arbitrary", "arbitrary")),
        name="diff_attn",
    )(qkv, qkv, qkv, lam_params, g_row)


def _rope_tables(seq):
    pos = jnp.arange(seq, dtype=F32)
    inv_freq = ROPE_THETA ** (-jnp.arange(0, ROT_DIM, 2, dtype=F32) / ROT_DIM)
    freqs = pos[:, None] * inv_freq[None, :]
    cos, sin = jnp.cos(freqs), jnp.sin(freqs)
    pad = DIFF_DH - ROT_DIM
    cos_t = jnp.concatenate([cos, cos, jnp.ones((seq, pad), F32)], axis=1)
    sin_t = jnp.concatenate([-sin, sin, jnp.zeros((seq, pad), F32)], axis=1)
    return cos_t, sin_t


def kernel(x, c, mod_w, mod_b, norm_mix_g, norm_mlp_g, a_w_in, a_conv_w, a_log, a_dt_bias, a_out_norm_g, a_w_out,
           kv_mod_w, kv_mod_b, kv_norm_g, kv_w, b_w_q, b_lambda, b_subln_g, b_w_out, mlp_w1, mlp_w2, final_g):
    bsz, seq, d = x.shape
    depth = mod_w.shape[0]
    assert depth == 2 and a_w_in.shape[0] == 1 and b_w_q.shape[0] == 1
    assert seq % TS_PROJ == 0 and seq % TS_SCAN == 0 and seq % TM_MLP == 0 and seq % TQ_ATTN == 0

    c_t = c.T
    mod = _mod_call(c_t, mod_w, mod_b[:, None, :]).reshape(depth, bsz, 6, d)
    kvmod = _mod_call(c_t, kv_mod_w[None], kv_mod_b[None, None, :]).reshape(bsz, 2, d)

    row = lambda v: v.reshape(1, -1).astype(F32)

    w_in = a_w_in[0]
    n_main = 4 * d
    w_main = w_in[:, :n_main].astype(BF16)
    w_ab = jnp.pad(w_in[:, n_main:], ((0, 0), (0, V7X_LANES - 2 * GDN_HEADS))).astype(BF16)
    ad = jnp.pad(jnp.stack([a_log[0], a_dt_bias[0]]).astype(F32), ((0, 0), (0, V7X_LANES - GDN_HEADS)))
    qkvz, gb = _gdn_inproj(x, mod[0], row(norm_mix_g[0]), w_main, w_ab, a_conv_w[0].astype(F32), ad)
    y0 = _gdn_scan(qkvz, gb, row(a_out_norm_g[0]))
    x = _mixer_out_mlp(x, y0, mod[0], a_w_out[0].astype(BF16), row(norm_mlp_g[0]),
                       mlp_w1[0].astype(BF16), mlp_w2[0].astype(BF16), row(final_g), False)

    lam_init = 0.8 - 0.6 * math.exp(-0.3 * 1)
    w_qkv = jnp.concatenate([b_w_q[0], kv_w], axis=1).astype(BF16)
    cos_t, sin_t = _rope_tables(seq)
    qkv = _attn_inproj(x, mod[1], kvmod, row(norm_mix_g[1]), row(kv_norm_g), w_qkv, cos_t, sin_t)
    y1 = _diff_attn(qkv, b_lambda[0].astype(F32), row(b_subln_g[0]), lam_init)
    x = _mixer_out_mlp(x, y1, mod[1], b_w_out[0].astype(BF16), row(norm_mlp_g[1]),
                       mlp_w1[1].astype(BF16), mlp_w2[1].astype(BF16), row(final_g), True)
    return x
```

```python
import functools
import math

import jax
import jax.numpy as jnp
from jax import lax
from jax.experimental import pallas as pl
from jax.experimental.pallas import tpu as pltpu

F32 = jnp.float32
BF16 = jnp.bfloat16
HIGHEST = lax.Precision.HIGHEST

V7X_LANES = 128
V7X_SUBLANES = 8
V7X_VMEM_LIMIT_BYTES = 56 * 1024 * 1024

EPS = 1e-6
GDN_HEADS = 8
GDN_DK = 128
GDN_CHUNK = 64
CONV_K = 4
DIFF_HEADS = 4
DIFF_DH = 128
ROT_DIM = DIFF_DH // 4
ROPE_THETA = 500000.0
LOG2E = 1.4426950408889634

TS_PROJ = 512
TS_SCAN = 512
TM_MLP = 512
TQ_ATTN = 1024
ATTN_ROW_TILE = 256
FF_CHUNK = 1024


def _params(sem, vmem=V7X_VMEM_LIMIT_BYTES, flags=None):
    return pltpu.CompilerParams(dimension_semantics=sem, vmem_limit_bytes=vmem, flags=flags)


def _sigmoid(x):
    return 1.0 / (1.0 + jnp.exp(-x))


def _silu(x):
    return x * _sigmoid(x)


def _softplus(x):
    return jnp.maximum(x, 0.0) + jnp.log1p(jnp.exp(-jnp.abs(x)))


def _dot(a, b, precision=None):
    return jnp.dot(a, b, preferred_element_type=F32, precision=precision)


def _dot_nt(a, b):
    return lax.dot_general(a, b, (((1,), (1,)), ((), ())), preferred_element_type=F32)


def _dot_tn(a, b):
    return lax.dot_general(a, b, (((0,), (0,)), ((), ())), preferred_element_type=F32)


def _split_bf16(x):
    hi = x.astype(BF16)
    return hi, (x - hi.astype(F32)).astype(BF16)


def _dot3(a, b):
    a_hi, a_lo = _split_bf16(a)
    b_hi, b_lo = _split_bf16(b)
    return _dot(jnp.concatenate([a_hi, a_lo, a_hi], axis=1),
                jnp.concatenate([b_hi, b_hi, b_lo], axis=0))


def _mod_kernel(ct_ref, w_ref, b_ref, o_ref, *, batch):
    cs = _silu(ct_ref[...])
    w = w_ref[0]
    for b in range(batch):
        o_ref[0, b:b + 1, :] = jnp.sum(w * cs[:, b:b + 1], axis=0, keepdims=True) + b_ref[0]


def _mod_call(c_t, w, bias, tn=2048):
    n_l, d, n = w.shape
    batch = c_t.shape[1]
    return pl.pallas_call(
        functools.partial(_mod_kernel, batch=batch),
        grid=(n_l, n // tn),
        in_specs=[
            pl.BlockSpec((d, batch), lambda l, j: (0, 0)),
            pl.BlockSpec((1, d, tn), lambda l, j: (l, 0, j)),
            pl.BlockSpec((1, 1, tn), lambda l, j: (l, 0, j)),
        ],
        out_specs=pl.BlockSpec((1, batch, tn), lambda l, j: (l, 0, j)),
        out_shape=jax.ShapeDtypeStruct((n_l, batch, n), F32),
        compiler_params=_params(("arbitrary", "arbitrary")),
        name="mod_vectors",
    )(c_t, w, bias)


def _modulated_norm(x, g_row, shift_row, scale_row):
    rstd = lax.rsqrt(jnp.mean(x * x, axis=-1, keepdims=True) + EPS)
    return (x * rstd) * (g_row * (1.0 + scale_row)) + shift_row


def _gdn_inproj_kernel(x_ref, mod_ref, g_ref, w_ref, wab_ref, cw_ref, ad_ref,
                       o_ref, gb_ref, h_ref, big_ref, carry_ref, *, ts):
    s = pl.program_id(1)
    j = pl.program_id(2)

    @pl.when(j == 0)
    def _():
        m = mod_ref[0]
        h = _modulated_norm(x_ref[0], g_ref[...], m[0:1], m[1:2])
        hb = h.astype(BF16)
        h_ref[...] = hb
        ab = _dot(hb, wab_ref[...])
        ad = ad_ref[...]
        g = -jnp.exp(ad[0:1]) * _softplus(ab + ad[1:2])
        lane = lax.broadcasted_iota(jnp.int32, ab.shape, 1)
        gb_ref[0] = jnp.where(lane < GDN_HEADS, g, _sigmoid(ab))

    proj = _dot(h_ref[...], w_ref[...])

    @pl.when(j == 3)
    def _():
        o_ref[0] = proj

    @pl.when(j < 3)
    def _():
        jj = jnp.minimum(j, 2)
        prev = carry_ref[jj]
        big_ref[0:V7X_SUBLANES] = jnp.where(s == 0, jnp.zeros_like(prev), prev)
        big_ref[V7X_SUBLANES:] = proj
        carry_ref[jj] = proj[ts - V7X_SUBLANES:]
        cw = cw_ref[...]
        base = V7X_SUBLANES - (CONV_K - 1)
        y = big_ref[base:base + ts] * cw[0:1]
        for t in range(1, CONV_K):
            y = y + big_ref[base + t:base + t + ts] * cw[t:t + 1]
        y = _silu(y)

        @pl.when(j == 2)
        def _():
            o_ref[0] = y

        @pl.when(j < 2)
        def _():
            for hh in range(GDN_HEADS):
                cs = slice(hh * GDN_DK, (hh + 1) * GDN_DK)
                yh = y[:, cs]
                o_ref[0, :, cs] = yh * lax.rsqrt(jnp.sum(yh * yh, axis=-1, keepdims=True) + EPS)


def _gdn_inproj(x, mod, g_row, w, wab, cw, ad, ts=TS_PROJ):
    bsz, seq, d = x.shape
    n_col = w.shape[1] // d
    return pl.pallas_call(
        functools.partial(_gdn_inproj_kernel, ts=ts),
        grid=(bsz, seq // ts, n_col),
        in_specs=[
            pl.BlockSpec((1, ts, d), lambda b, s, j: (b, s, 0)),
            pl.BlockSpec((1, 6, d), lambda b, s, j: (b, 0, 0)),
            pl.BlockSpec((1, d), lambda b, s, j: (0, 0)),
            pl.BlockSpec((d, d), lambda b, s, j: (0, j)),
            pl.BlockSpec((d, V7X_LANES), lambda b, s, j: (0, 0)),
            pl.BlockSpec((CONV_K, d), lambda b, s, j: (0, jnp.minimum(j, 2))),
            pl.BlockSpec((2, V7X_LANES), lambda b, s, j: (0, 0)),
        ],
        out_specs=[
            pl.BlockSpec((1, ts, d), lambda b, s, j: (b, s, j)),
            pl.BlockSpec((1, ts, V7X_LANES), lambda b, s, j: (b, s, 0)),
        ],
        out_shape=[
            jax.ShapeDtypeStruct((bsz, seq, n_col * d), F32),
            jax.ShapeDtypeStruct((bsz, seq, V7X_LANES), F32),
        ],
        scratch_shapes=[
            pltpu.VMEM((ts, d), BF16),
            pltpu.VMEM((ts + V7X_SUBLANES, d), F32),
            pltpu.VMEM((3, V7X_SUBLANES, d), F32),
        ],
        compiler_params=_params(("arbitrary", "arbitrary", "arbitrary")),
        name="gdn_inproj",
    )(x, mod, g_row, w, wab, cw, ad)


def _gdn_scan_kernel(q_ref, k_ref, v_ref, z_ref, gb_ref, og_ref, y_ref, state_ref, *, ts):
    c_sz = GDN_CHUNK

    @pl.when(pl.program_id(1) == 0)
    def _():
        state_ref[...] = jnp.zeros_like(state_ref)

    row = lax.broadcasted_iota(jnp.int32, (c_sz, c_sz), 0)
    col = lax.broadcasted_iota(jnp.int32, (c_sz, c_sz), 1)
    incl = row >= col
    strict = row > col
    ltri = incl.astype(F32)
    eye = (row == col).astype(F32)
    og = og_ref[...]
    q_scale = GDN_DK ** -0.5

    def chunk(c, carry):
        r0 = pl.multiple_of(c * c_sz, c_sz)
        rows = pl.ds(r0, c_sz)
        gbc = gb_ref[0, rows, :]
        gcum = _dot(ltri, gbc, HIGHEST)
        gcum_t = gcum.T
        heads = range(GDN_HEADS)
        cols = [slice(h * GDN_DK, (h + 1) * GDN_DK) for h in heads]
        qh = [q_ref[0, rows, cols[h]] * q_scale for h in heads]
        kh = [k_ref[0, rows, cols[h]] for h in heads]
        gc = [gcum[:, h:h + 1] for h in heads]
        beta = [gbc[:, GDN_HEADS + h:GDN_HEADS + h + 1] for h in heads]
        glast = [gcum[c_sz - 1:c_sz, h:h + 1] for h in heads]
        eg = [jnp.exp(gc[h]) for h in heads]
        decay = [jnp.where(incl, jnp.exp(jnp.where(incl, gc[h] - gcum_t[h:h + 1, :], 0.0)), 0.0)
                 for h in heads]
        kb = [kh[h] * beta[h] for h in heads]
        kk = [_dot_nt(jnp.concatenate([kb[h], qh[h]], axis=0).astype(BF16), kh[h].astype(BF16))
              for h in heads]
        a_mat = [jnp.where(strict, kk[h][:c_sz] * decay[h], 0.0) for h in heads]
        qk = [(kk[h][c_sz:] * decay[h]).astype(BF16) for h in heads]
        x_mat = [eye - a_mat[h] for h in heads]
        p_mat = [_dot3(a_mat[h], a_mat[h]) for h in heads]
        for _ in range(4):
            xp = [_dot3(jnp.concatenate([x_mat[h], p_mat[h]], axis=0), p_mat[h]) for h in heads]
            x_mat = [x_mat[h] + xp[h][:c_sz] for h in heads]
            p_mat = [xp[h][c_sz:] for h in heads]
        xl = [_dot3(x_mat[h], p_mat[h]) for h in heads]
        x_mat = [x_mat[h] + xl[h] for h in heads]
        sol = [_dot3(x_mat[h],
                     jnp.concatenate([v_ref[0, rows, cols[h]] * beta[h], kb[h] * eg[h]], axis=1))
               for h in heads]
        st = [state_ref[h] for h in heads]
        wq = [_dot(jnp.concatenate([sol[h][:, GDN_DK:], qh[h] * eg[h]], axis=0).astype(BF16),
                   st[h].astype(BF16)) for h in heads]
        v_new = [(sol[h][:, :GDN_DK] - wq[h][:c_sz]).astype(BF16) for h in heads]
        o = [wq[h][c_sz:] + _dot(qk[h], v_new[h]) for h in heads]
        for h in heads:
            kdec = kh[h] * jnp.exp(glast[h] - gc[h])
            state_ref[h] = st[h] * jnp.exp(glast[h]) + _dot_tn(kdec.astype(BF16), v_new[h])
        for h in heads:
            on = o[h] * lax.rsqrt(jnp.mean(o[h] * o[h], axis=-1, keepdims=True) + EPS) * og
            y_ref[0, rows, cols[h]] = (on * _silu(z_ref[0, rows, cols[h]])).astype(BF16)
        return carry

    lax.fori_loop(0, ts // c_sz, chunk, 0)


def _gdn_scan(qkvz, gb, og_row, ts=TS_SCAN):
    bsz, seq, _ = qkvz.shape
    d = GDN_HEADS * GDN_DK
    col = lambda j: pl.BlockSpec((1, ts, d), lambda b, s: (b, s, j))
    return pl.pallas_call(
        functools.partial(_gdn_scan_kernel, ts=ts),
        grid=(bsz, seq // ts),
        in_specs=[col(0), col(1), col(2), col(3),
                  pl.BlockSpec((1, ts, V7X_LANES), lambda b, s: (b, s, 0)),
                  pl.BlockSpec((1, GDN_DK), lambda b, s: (0, 0))],
        out_specs=pl.BlockSpec((1, ts, d), lambda b, s: (b, s, 0)),
        out_shape=jax.ShapeDtypeStruct((bsz, seq, d), BF16),
        scratch_shapes=[pltpu.VMEM((GDN_HEADS, GDN_DK, GDN_DK), F32)],
        compiler_params=_params(("arbitrary", "arbitrary")),
        name="gdn_scan",
    )(qkvz, qkvz, qkvz, qkvz, gb, og_row)


def _mixer_out_mlp_kernel(x_ref, y_ref, mod_ref, wo_ref, g_ref, w1_ref, w2_ref, fg_ref, o_ref,
                          *, final_norm):
    m = mod_ref[0]
    x1 = x_ref[0] + m[2:3] * _dot(y_ref[0], wo_ref[...])
    h = _modulated_norm(x1, g_ref[...], m[3:4], m[4:5]).astype(BF16)
    d_ff = w1_ref.shape[1]
    acc = None
    for c in range(d_ff // FF_CHUNK):
        cs = slice(c * FF_CHUNK, (c + 1) * FF_CHUNK)
        hid = jnp.maximum(_dot(h, w1_ref[:, cs]), 0.0)
        part = _dot((hid * hid).astype(BF16), w2_ref[cs, :])
        acc = part if acc is None else acc + part
    x2 = x1 + m[5:6] * acc
    if final_norm:
        x2 = x2 * lax.rsqrt(jnp.mean(x2 * x2, axis=-1, keepdims=True) + EPS) * fg_ref[...]
    o_ref[0] = x2


def _mixer_out_mlp(x, y, mod, wo, g_row, w1, w2, fg_row, final_norm, tm=TM_MLP):
    bsz, seq, d = x.shape
    d_ff = w1.shape[1]
    const = lambda shape: pl.BlockSpec(shape, lambda b, s: (0,) * len(shape),
                                       pipeline_mode=pl.Buffered(1))
    return pl.pallas_call(
        functools.partial(_mixer_out_mlp_kernel, final_norm=final_norm),
        grid=(bsz, seq // tm),
        in_specs=[
            pl.BlockSpec((1, tm, d), lambda b, s: (b, s, 0)),
            pl.BlockSpec((1, tm, d), lambda b, s: (b, s, 0)),
            pl.BlockSpec((1, 6, d), lambda b, s: (b, 0, 0)),
            const((d, d)),
            const((1, d)),
            const((d, d_ff)),
            const((d_ff, d)),
            const((1, d)),
        ],
        out_specs=pl.BlockSpec((1, tm, d), lambda b, s: (b, s, 0)),
        out_shape=jax.ShapeDtypeStruct((bsz, seq, d), F32),
        compiler_params=_params(("arbitrary", "arbitrary")),
        name="mixer_out_mlp",
    )(x, y, mod, wo, g_row, w1, w2, fg_row)


def _rope(y, cos_t, sin_t):
    half = ROT_DIM // 2
    lane = lax.broadcasted_iota(jnp.int32, cos_t.shape, 1)
    outs = []
    for gidx in range(y.shape[1] // DIFF_DH):
        yh = y[:, gidx * DIFF_DH:(gidx + 1) * DIFF_DH]
        rot = jnp.where(lane < half, pltpu.roll(yh, DIFF_DH - half, 1), pltpu.roll(yh, half, 1))
        outs.append(yh * cos_t + rot * sin_t)
    return outs


def _attn_inproj_kernel(x_ref, mod_ref, kvmod_ref, gq_ref, gkv_ref, w_ref, cos_ref, sin_ref,
                        o_ref, hq_ref, hk_ref):
    j = pl.program_id(2)

    @pl.when(j == 0)
    def _():
        x = x_ref[0]
        xn = x * lax.rsqrt(jnp.mean(x * x, axis=-1, keepdims=True) + EPS)
        m = mod_ref[0]
        km = kvmod_ref[0]
        hq_ref[...] = (xn * (gq_ref[...] * (1.0 + m[1:2])) + m[0:1]).astype(BF16)
        hk_ref[...] = (xn * (gkv_ref[...] * (1.0 + km[1:2])) + km[0:1]).astype(BF16)

    @pl.when(j == 0)
    def _():
        y = _dot(hq_ref[...], w_ref[...])
        q_scale = (DIFF_DH ** -0.5) * LOG2E
        for gidx, yr in enumerate(_rope(y, cos_ref[...], sin_ref[...])):
            o_ref[0, :, gidx * DIFF_DH:(gidx + 1) * DIFF_DH] = (yr * q_scale).astype(BF16)

    @pl.when(j == 1)
    def _():
        y = _dot(hk_ref[...], w_ref[...])
        for gidx, yr in enumerate(_rope(y, cos_ref[...], sin_ref[...])):
            o_ref[0, :, gidx * DIFF_DH:(gidx + 1) * DIFF_DH] = yr.astype(BF16)

    @pl.when(j == 2)
    def _():
        o_ref[0] = _dot(hk_ref[...], w_ref[...]).astype(BF16)


def _attn_inproj(x, mod, kvmod, gq_row, gkv_row, w, cos_t, sin_t, ts=TS_PROJ):
    bsz, seq, d = x.shape
    n_col = w.shape[1] // d
    return pl.pallas_call(
        _attn_inproj_kernel,
        grid=(bsz, seq // ts, n_col),
        in_specs=[
            pl.BlockSpec((1, ts, d), lambda b, s, j: (b, s, 0)),
            pl.BlockSpec((1, 6, d), lambda b, s, j: (b, 0, 0)),
            pl.BlockSpec((1, 2, d), lambda b, s, j: (b, 0, 0)),
            pl.BlockSpec((1, d), lambda b, s, j: (0, 0)),
            pl.BlockSpec((1, d), lambda b, s, j: (0, 0)),
            pl.BlockSpec((d, d), lambda b, s, j: (0, j)),
            pl.BlockSpec((ts, DIFF_DH), lambda b, s, j: (s, 0)),
            pl.BlockSpec((ts, DIFF_DH), lambda b, s, j: (s, 0)),
        ],
        out_specs=pl.BlockSpec((1, ts, d), lambda b, s, j: (b, s, j)),
        out_shape=jax.ShapeDtypeStruct((bsz, seq, n_col * d), BF16),
        scratch_shapes=[pltpu.VMEM((ts, d), BF16), pltpu.VMEM((ts, d), BF16)],
        compiler_params=_params(("arbitrary", "arbitrary", "arbitrary")),
        name="attn_inproj",
    )(x, mod, kvmod, gq_row, gkv_row, w, cos_t, sin_t)


def _diff_attn_kernel(q_ref, k_ref, v_ref, lam_ref, g_ref, o_ref, m_ref, l_ref, acc_ref,
                      s_ref, mx_ref, *, tq, tk, lam_init):
    qi = pl.program_id(2)
    m_ref[...] = jnp.full(m_ref.shape, -jnp.inf, F32)
    l_ref[...] = jnp.zeros_like(l_ref)
    acc_ref[...] = jnp.zeros_like(acc_ref)

    rt = ATTN_ROW_TILE
    n_rt = tq // rt

    def chunks(width):
        return [slice(j * V7X_LANES, (j + 1) * V7X_LANES) for j in range(width // V7X_LANES)]

    def scores_rows(kblk, slot, rq):
        rk = pl.ds(pl.multiple_of(kblk * tk, tk), tk)
        for mi in range(2):
            cs = slice(mi * DIFF_DH, (mi + 1) * DIFF_DH)
            s = _dot_nt(q_ref[0, rq, cs], k_ref[0, rk, cs])
            s_ref[slot, mi, rq, :] = s
            mx_ref[slot, mi, rq, :] = functools.reduce(jnp.maximum, [s[:, c] for c in chunks(tk)])

    def softmax_rows(slot, rq, width, diag_row0):
        if diag_row0 is not None:
            r = lax.broadcasted_iota(jnp.int32, (rt, V7X_LANES), 0)
            c = lax.broadcasted_iota(jnp.int32, (rt, V7X_LANES), 1)
            diff = r - c + diag_row0
        out = []
        for mi in range(2):
            sc = [s_ref[slot, mi, rq, c_] for c_ in chunks(width)]
            if diag_row0 is not None:
                sc = [jnp.where(diff >= c_.start, x, -jnp.inf) for c_, x in zip(chunks(width), sc)]
                mx = functools.reduce(jnp.maximum, sc)
            else:
                mx = mx_ref[slot, mi, rq, :]
            m_old = m_ref[mi, rq, :]
            m_new = jnp.maximum(m_old, jnp.max(mx, axis=-1, keepdims=True))
            alpha = jnp.exp2(m_old - m_new)
            m_ref[mi, rq, :] = m_new
            ps = [jnp.exp2(x - m_new) for x in sc]
            l_ref[mi, rq, :] = alpha * l_ref[mi, rq, :] + functools.reduce(jnp.add, ps)
            out.append((alpha, jnp.concatenate(ps, axis=1).astype(BF16)))
        return out

    def pv_rows(mi, rq, alpha, p, v):
        acc_ref[mi, rq, :] = jnp.concatenate([alpha, alpha], axis=1) * acc_ref[mi, rq, :] + _dot(p, v)

    def row_tile(t):
        return pl.ds(pl.multiple_of(t * rt, rt), rt)

    def first_scores(t, carry):
        scores_rows(0, 0, row_tile(t))
        return carry

    lax.fori_loop(0, n_rt, first_scores, 0)

    def visible_block(i, carry):
        slot = i & 1
        v = v_ref[0, pl.ds(pl.multiple_of(i * tk, tk), tk), :]

        def tile(t, c):
            rq = row_tile(t)
            for mi, (alpha, p) in enumerate(softmax_rows(slot, rq, tk, None)):
                pv_rows(mi, rq, alpha, p, v)
            scores_rows(i + 1, 1 - slot, rq)
            return c

        return lax.fori_loop(0, n_rt, tile, carry)

    lax.fori_loop(0, qi, visible_block, 0)

    diag_slot = lax.rem(qi, 2)
    kdiag = pl.multiple_of(qi * tk, tk)
    for t in range(n_rt):
        rq = pl.ds(t * rt, rt)
        width = (t + 1) * rt
        v = v_ref[0, pl.ds(kdiag, width), :]
        for mi, (alpha, p) in enumerate(softmax_rows(diag_slot, rq, width, t * rt)):
            pv_rows(mi, rq, alpha, p, v)

    lp = lam_ref[...]
    lam = (jnp.exp(jnp.sum(lp[0:1] * lp[1:2], axis=-1, keepdims=True))
           - jnp.exp(jnp.sum(lp[2:3] * lp[3:4], axis=-1, keepdims=True)) + lam_init)
    l0 = jnp.sum(l_ref[0], axis=-1, keepdims=True)
    l1 = jnp.sum(l_ref[1], axis=-1, keepdims=True)
    o = acc_ref[0] / l0 - lam * (acc_ref[1] / l1)
    on = o * lax.rsqrt(jnp.mean(o * o, axis=-1, keepdims=True) + EPS) * g_ref[...]
    o_ref[0] = (on * (1.0 - lam_init)).astype(BF16)


def _diff_attn(qkv, lam_params, g_row, lam_init, tq=TQ_ATTN):
    bsz, seq, _ = qkv.shape
    hw = 2 * DIFF_DH
    tk = tq
    return pl.pallas_call(
        functools.partial(_diff_attn_kernel, tq=tq, tk=tk, lam_init=lam_init),
        grid=(bsz, DIFF_HEADS, seq // tq),
        in_specs=[
            pl.BlockSpec((1, tq, hw), lambda b, h, i: (b, i, h)),
            pl.BlockSpec((1, seq, hw), lambda b, h, i: (b, 0, DIFF_HEADS + h),
                         pipeline_mode=pl.Buffered(1)),
            pl.BlockSpec((1, seq, hw), lambda b, h, i: (b, 0, 2 * DIFF_HEADS + h),
                         pipeline_mode=pl.Buffered(1)),
            pl.BlockSpec((4, DIFF_DH), lambda b, h, i: (0, 0)),
            pl.BlockSpec((1, hw), lambda b, h, i: (0, 0)),
        ],
        out_specs=pl.BlockSpec((1, tq, hw), lambda b, h, i: (b, i, h)),
        out_shape=jax.ShapeDtypeStruct((bsz, seq, DIFF_HEADS * hw), BF16),
        scratch_shapes=[
            pltpu.VMEM((2, tq, V7X_LANES), F32),
            pltpu.VMEM((2, tq, V7X_LANES), F32),
            pltpu.VMEM((2, tq, hw), F32),
            pltpu.VMEM((2, 2, tq, tk), F32),
            pltpu.VMEM((2, 2, tq, V7X_LANES), F32),
        ],
        compiler_params=_params(("arbitrary", "arbitrary", "arbitrary")),
        name="diff_attn",
    )(qkv, qkv, qkv, lam_params, g_row)


def _rope_tables(seq):
    pos = jnp.arange(seq, dtype=F32)
    inv_freq = ROPE_THETA ** (-jnp.arange(0, ROT_DIM, 2, dtype=F32) / ROT_DIM)
    freqs = pos[:, None] * inv_freq[None, :]
    cos, sin = jnp.cos(freqs), jnp.sin(freqs)
    pad = DIFF_DH - ROT_DIM
    cos_t = jnp.concatenate([cos, cos, jnp.ones((seq, pad), F32)], axis=1)
    sin_t = jnp.concatenate([-sin, sin, jnp.zeros((seq, pad), F32)], axis=1)
    return cos_t, sin_t


def kernel(x, c, mod_w, mod_b, norm_mix_g, norm_mlp_g, a_w_in, a_conv_w, a_log, a_dt_bias, a_out_norm_g, a_w_out,
           kv_mod_w, kv_mod_b, kv_norm_g, kv_w, b_w_q, b_lambda, b_subln_g, b_w_out, mlp_w1, mlp_w2, final_g):
    bsz, seq, d = x.shape
    depth = mod_w.shape[0]
    assert depth == 2 and a_w_in.shape[0] == 1 and b_w_q.shape[0] == 1
    assert seq % TS_PROJ == 0 and seq % TS_SCAN == 0 and seq % TM_MLP == 0 and seq % TQ_ATTN == 0

    c_t = c.T
    mod = _mod_call(c_t, mod_w, mod_b[:, None, :]).reshape(depth, bsz, 6, d)
    kvmod = _mod_call(c_t, kv_mod_w[None], kv_mod_b[None, None, :]).reshape(bsz, 2, d)

    row = lambda v: v.reshape(1, -1).astype(F32)

    w_in = a_w_in[0]
    n_main = 4 * d
    w_main = w_in[:, :n_main].astype(BF16)
    w_ab = jnp.pad(w_in[:, n_main:], ((0, 0), (0, V7X_LANES - 2 * GDN_HEADS))).astype(BF16)
    ad = jnp.pad(jnp.stack([a_log[0], a_dt_bias[0]]).astype(F32), ((0, 0), (0, V7X_LANES - GDN_HEADS)))
    qkvz, gb = _gdn_inproj(x, mod[0], row(norm_mix_g[0]), w_main, w_ab, a_conv_w[0].astype(F32), ad)
    y0 = _gdn_scan(qkvz, gb, row(a_out_norm_g[0]))
    x = _mixer_out_mlp(x, y0, mod[0], a_w_out[0].astype(BF16), row(norm_mlp_g[0]),
                       mlp_w1[0].astype(BF16), mlp_w2[0].astype(BF16), row(final_g), False)

    lam_init = 0.8 - 0.6 * math.exp(-0.3 * 1)
    w_qkv = jnp.concatenate([b_w_q[0], kv_w], axis=1).astype(BF16)
    cos_t, sin_t = _rope_tables(seq)
    qkv = _attn_inproj(x, mod[1], kvmod, row(norm_mix_g[1]), row(kv_norm_g), w_qkv, cos_t, sin_t)
    y1 = _diff_attn(qkv, b_lambda[0].astype(F32), row(b_subln_g[0]), lam_init)
    x = _mixer_out_mlp(x, y1, mod[1], b_w_out[0].astype(BF16), row(norm_mlp_g[1]),
                       mlp_w1[1].astype(BF16), mlp_w2[1].astype(BF16), row(final_g), True)
    return x
```

```python
import functools
import math

import jax
import jax.numpy as jnp
from jax import lax
from jax.experimental import pallas as pl
from jax.experimental.pallas import tpu as pltpu

F32 = jnp.float32
BF16 = jnp.bfloat16
HIGHEST = lax.Precision.HIGHEST

V7X_LANES = 128
V7X_SUBLANES = 8
V7X_VMEM_LIMIT_BYTES = 56 * 1024 * 1024

EPS = 1e-6
GDN_HEADS = 8
GDN_DK = 128
GDN_CHUNK = 64
GDN_CHUNKS_PER_STEP = 2
CONV_K = 4
DIFF_HEADS = 4
DIFF_DH = 128
ROT_DIM = DIFF_DH // 4
ROPE_THETA = 500000.0
LOG2E = 1.4426950408889634

TS_PROJ = 512
TS_SCAN = 512
TM_MLP = 512
TQ_ATTN = 512
TK_ATTN = 2048
ATTN_ROW_TILE = 256
FF_CHUNK = 1024


def _params(sem, vmem=V7X_VMEM_LIMIT_BYTES, flags=None):
    return pltpu.CompilerParams(dimension_semantics=sem, vmem_limit_bytes=vmem, flags=flags)


def _sigmoid(x):
    return 1.0 / (1.0 + jnp.exp(-x))


def _silu(x):
    return x * _sigmoid(x)


def _softplus(x):
    return jnp.maximum(x, 0.0) + jnp.log1p(jnp.exp(-jnp.abs(x)))


def _dot(a, b, precision=None):
    return jnp.dot(a, b, preferred_element_type=F32, precision=precision)


def _dot_nt(a, b):
    return lax.dot_general(a, b, (((1,), (1,)), ((), ())), preferred_element_type=F32)


def _dot_tn(a, b):
    return lax.dot_general(a, b, (((0,), (0,)), ((), ())), preferred_element_type=F32)


def _split_bf16(x):
    hi = x.astype(BF16)
    return hi, (x - hi.astype(F32)).astype(BF16)


def _dot3(a, b):
    a_hi, a_lo = _split_bf16(a)
    b_hi, b_lo = _split_bf16(b)
    return _dot(jnp.concatenate([a_hi, a_lo, a_hi], axis=1),
                jnp.concatenate([b_hi, b_hi, b_lo], axis=0))


def _mod_kernel(ct_ref, w_ref, b_ref, o_ref, *, batch):
    cs = _silu(ct_ref[...])
    w = w_ref[0]
    for b in range(batch):
        o_ref[0, b:b + 1, :] = jnp.sum(w * cs[:, b:b + 1], axis=0, keepdims=True) + b_ref[0]


def _mod_call(c_t, w, bias, tn=2048):
    n_l, d, n = w.shape
    batch = c_t.shape[1]
    return pl.pallas_call(
        functools.partial(_mod_kernel, batch=batch),
        grid=(n_l, n // tn),
        in_specs=[
            pl.BlockSpec((d, batch), lambda l, j: (0, 0)),
            pl.BlockSpec((1, d, tn), lambda l, j: (l, 0, j)),
            pl.BlockSpec((1, 1, tn), lambda l, j: (l, 0, j)),
        ],
        out_specs=pl.BlockSpec((1, batch, tn), lambda l, j: (l, 0, j)),
        out_shape=jax.ShapeDtypeStruct((n_l, batch, n), F32),
        compiler_params=_params(("arbitrary", "arbitrary")),
        name="mod_vectors",
    )(c_t, w, bias)


def _modulated_norm(x, g_row, shift_row, scale_row):
    rstd = lax.rsqrt(jnp.mean(x * x, axis=-1, keepdims=True) + EPS)
    return (x * rstd) * (g_row * (1.0 + scale_row)) + shift_row


def _gdn_inproj_kernel(x_ref, mod_ref, g_ref, w_ref, wab_ref, cw_ref, ad_ref,
                       o_ref, gb_ref, h_ref, big_ref, carry_ref, *, ts):
    s = pl.program_id(1)
    j = pl.program_id(2)

    @pl.when(j == 0)
    def _():
        m = mod_ref[0]
        h = _modulated_norm(x_ref[0], g_ref[...], m[0:1], m[1:2])
        hb = h.astype(BF16)
        h_ref[...] = hb
        ab = _dot(hb, wab_ref[...])
        ad = ad_ref[...]
        g = -jnp.exp(ad[0:1]) * _softplus(ab + ad[1:2])
        lane = lax.broadcasted_iota(jnp.int32, ab.shape, 1)
        gb_ref[0] = jnp.where(lane < GDN_HEADS, g, _sigmoid(ab))

    proj = _dot(h_ref[...], w_ref[...])

    @pl.when(j == 3)
    def _():
        o_ref[0] = proj

    @pl.when(j < 3)
    def _():
        jj = jnp.minimum(j, 2)
        prev = carry_ref[jj]
        big_ref[0:V7X_SUBLANES] = jnp.where(s == 0, jnp.zeros_like(prev), prev)
        big_ref[V7X_SUBLANES:] = proj
        carry_ref[jj] = proj[ts - V7X_SUBLANES:]
        cw = cw_ref[...]
        base = V7X_SUBLANES - (CONV_K - 1)
        y = big_ref[base:base + ts] * cw[0:1]
        for t in range(1, CONV_K):
            y = y + big_ref[base + t:base + t + ts] * cw[t:t + 1]
        y = _silu(y)

        @pl.when(j == 2)
        def _():
            o_ref[0] = y

        @pl.when(j < 2)
        def _():
            for hh in range(GDN_HEADS):
                cs = slice(hh * GDN_DK, (hh + 1) * GDN_DK)
                yh = y[:, cs]
                o_ref[0, :, cs] = yh * lax.rsqrt(jnp.sum(yh * yh, axis=-1, keepdims=True) + EPS)


def _gdn_inproj(x, mod, g_row, w, wab, cw, ad, ts=TS_PROJ):
    bsz, seq, d = x.shape
    n_col = w.shape[1] // d
    return pl.pallas_call(
        functools.partial(_gdn_inproj_kernel, ts=ts),
        grid=(bsz, seq // ts, n_col),
        in_specs=[
            pl.BlockSpec((1, ts, d), lambda b, s, j: (b, s, 0)),
            pl.BlockSpec((1, 6, d), lambda b, s, j: (b, 0, 0)),
            pl.BlockSpec((1, d), lambda b, s, j: (0, 0)),
            pl.BlockSpec((d, d), lambda b, s, j: (0, j)),
            pl.BlockSpec((d, V7X_LANES), lambda b, s, j: (0, 0)),
            pl.BlockSpec((CONV_K, d), lambda b, s, j: (0, jnp.minimum(j, 2))),
            pl.BlockSpec((2, V7X_LANES), lambda b, s, j: (0, 0)),
        ],
        out_specs=[
            pl.BlockSpec((1, ts, d), lambda b, s, j: (b, s, j)),
            pl.BlockSpec((1, ts, V7X_LANES), lambda b, s, j: (b, s, 0)),
        ],
        out_shape=[
            jax.ShapeDtypeStruct((bsz, seq, n_col * d), F32),
            jax.ShapeDtypeStruct((bsz, seq, V7X_LANES), F32),
        ],
        scratch_shapes=[
            pltpu.VMEM((ts, d), BF16),
            pltpu.VMEM((ts + V7X_SUBLANES, d), F32),
            pltpu.VMEM((3, V7X_SUBLANES, d), F32),
        ],
        compiler_params=_params(("arbitrary", "arbitrary", "arbitrary")),
        name="gdn_inproj",
    )(x, mod, g_row, w, wab, cw, ad)


def _gdn_scan_kernel(q_ref, k_ref, v_ref, z_ref, gb_ref, og_ref, y_ref, state_ref, *, ts):
    c_sz = GDN_CHUNK

    @pl.when(pl.program_id(1) == 0)
    def _():
        state_ref[...] = jnp.zeros_like(state_ref)

    row = lax.broadcasted_iota(jnp.int32, (c_sz, c_sz), 0)
    col = lax.broadcasted_iota(jnp.int32, (c_sz, c_sz), 1)
    incl = row >= col
    strict = row > col
    ltri = incl.astype(F32)
    eye = (row == col).astype(F32)
    og = og_ref[...]
    q_scale = GDN_DK ** -0.5

    heads = range(GDN_HEADS)
    cols = [slice(h * GDN_DK, (h + 1) * GDN_DK) for h in heads]
    n_sub = GDN_CHUNKS_PER_STEP

    def step(c, carry):
        rows, gbc, gcum, gcum_t = [], [], [], []
        for ci in range(n_sub):
            r0 = pl.multiple_of((c * n_sub + ci) * c_sz, c_sz)
            rows.append(pl.ds(r0, c_sz))
            gbc.append(gb_ref[0, rows[ci], :])
            gcum.append(_dot(ltri, gbc[ci], HIGHEST))
            gcum_t.append(gcum[ci].T)
        streams = [(ci, h) for ci in range(n_sub) for h in heads]
        qh = {s: q_ref[0, rows[s[0]], cols[s[1]]] * q_scale for s in streams}
        kh = {s: k_ref[0, rows[s[0]], cols[s[1]]] for s in streams}
        gc = {(ci, h): gcum[ci][:, h:h + 1] for ci, h in streams}
        beta = {(ci, h): gbc[ci][:, GDN_HEADS + h:GDN_HEADS + h + 1] for ci, h in streams}
        glast = {(ci, h): gcum[ci][c_sz - 1:c_sz, h:h + 1] for ci, h in streams}
        eg = {s: jnp.exp(gc[s]) for s in streams}
        decay = {(ci, h): jnp.where(incl, jnp.exp(jnp.where(incl, gc[(ci, h)] - gcum_t[ci][h:h + 1, :],
                                                           0.0)), 0.0) for ci, h in streams}
        kb = {s: kh[s] * beta[s] for s in streams}
        kk = {s: _dot_nt(jnp.concatenate([kb[s], qh[s]], axis=0).astype(BF16), kh[s].astype(BF16))
              for s in streams}
        a_mat = {s: jnp.where(strict, kk[s][:c_sz] * decay[s], 0.0) for s in streams}
        qk = {s: (kk[s][c_sz:] * decay[s]).astype(BF16) for s in streams}
        x_mat = {s: eye - a_mat[s] for s in streams}
        p_mat = {s: _dot3(a_mat[s], a_mat[s]) for s in streams}
        for _ in range(4):
            xp = {s: _dot3(jnp.concatenate([x_mat[s], p_mat[s]], axis=0), p_mat[s]) for s in streams}
            x_mat = {s: x_mat[s] + xp[s][:c_sz] for s in streams}
            p_mat = {s: xp[s][c_sz:] for s in streams}
        xl = {s: _dot3(x_mat[s], p_mat[s]) for s in streams}
        x_mat = {s: x_mat[s] + xl[s] for s in streams}
        sol = {s: _dot3(x_mat[s], jnp.concatenate([v_ref[0, rows[s[0]], cols[s[1]]] * beta[s],
                                                   kb[s] * eg[s]], axis=1))
               for s in streams}
        wq_lhs = {s: jnp.concatenate([sol[s][:, GDN_DK:], qh[s] * eg[s]], axis=0).astype(BF16)
                  for s in streams}
        kdec = {s: (kh[s] * jnp.exp(glast[s] - gc[s])).astype(BF16) for s in streams}
        for ci in range(n_sub):
            st = [state_ref[h] for h in heads]
            wq = [_dot(wq_lhs[(ci, h)], st[h].astype(BF16)) for h in heads]
            v_new = [(sol[(ci, h)][:, :GDN_DK] - wq[h][:c_sz]).astype(BF16) for h in heads]
            o = [wq[h][c_sz:] + _dot(qk[(ci, h)], v_new[h]) for h in heads]
            for h in heads:
                state_ref[h] = st[h] * jnp.exp(glast[(ci, h)]) + _dot_tn(kdec[(ci, h)], v_new[h])
            for h in heads:
                on = o[h] * lax.rsqrt(jnp.mean(o[h] * o[h], axis=-1, keepdims=True) + EPS) * og
                y_ref[0, rows[ci], cols[h]] = (on * _silu(z_ref[0, rows[ci], cols[h]])).astype(BF16)
        return carry

    lax.fori_loop(0, ts // (c_sz * n_sub), step, 0)


def _gdn_scan(qkvz, gb, og_row, ts=TS_SCAN):
    bsz, seq, _ = qkvz.shape
    d = GDN_HEADS * GDN_DK
    col = lambda j: pl.BlockSpec((1, ts, d), lambda b, s: (b, s, j))
    return pl.pallas_call(
        functools.partial(_gdn_scan_kernel, ts=ts),
        grid=(bsz, seq // ts),
        in_specs=[col(0), col(1), col(2), col(3),
                  pl.BlockSpec((1, ts, V7X_LANES), lambda b, s: (b, s, 0)),
                  pl.BlockSpec((1, GDN_DK), lambda b, s: (0, 0))],
        out_specs=pl.BlockSpec((1, ts, d), lambda b, s: (b, s, 0)),
        out_shape=jax.ShapeDtypeStruct((bsz, seq, d), BF16),
        scratch_shapes=[pltpu.VMEM((GDN_HEADS, GDN_DK, GDN_DK), F32)],
        compiler_params=_params(("arbitrary", "arbitrary")),
        name="gdn_scan",
    )(qkvz, qkvz, qkvz, qkvz, gb, og_row)


def _mixer_out_mlp_kernel(x_ref, y_ref, mod_ref, wo_ref, g_ref, w1_ref, w2_ref, fg_ref, o_ref,
                          *, final_norm):
    m = mod_ref[0]
    x1 = x_ref[0] + m[2:3] * _dot(y_ref[0], wo_ref[...])
    h = _modulated_norm(x1, g_ref[...], m[3:4], m[4:5]).astype(BF16)
    d_ff = w1_ref.shape[1]
    acc = None
    for c in range(d_ff // FF_CHUNK):
        cs = slice(c * FF_CHUNK, (c + 1) * FF_CHUNK)
        hid = jnp.maximum(_dot(h, w1_ref[:, cs]), 0.0)
        part = _dot((hid * hid).astype(BF16), w2_ref[cs, :])
        acc = part if acc is None else acc + part
    x2 = x1 + m[5:6] * acc
    if final_norm:
        x2 = x2 * lax.rsqrt(jnp.mean(x2 * x2, axis=-1, keepdims=True) + EPS) * fg_ref[...]
    o_ref[0] = x2


def _mixer_out_mlp(x, y, mod, wo, g_row, w1, w2, fg_row, final_norm, tm=TM_MLP):
    bsz, seq, d = x.shape
    d_ff = w1.shape[1]
    const = lambda shape: pl.BlockSpec(shape, lambda b, s: (0,) * len(shape),
                                       pipeline_mode=pl.Buffered(1))
    return pl.pallas_call(
        functools.partial(_mixer_out_mlp_kernel, final_norm=final_norm),
        grid=(bsz, seq // tm),
        in_specs=[
            pl.BlockSpec((1, tm, d), lambda b, s: (b, s, 0)),
            pl.BlockSpec((1, tm, d), lambda b, s: (b, s, 0)),
            pl.BlockSpec((1, 6, d), lambda b, s: (b, 0, 0)),
            const((d, d)),
            const((1, d)),
            const((d, d_ff)),
            const((d_ff, d)),
            const((1, d)),
        ],
        out_specs=pl.BlockSpec((1, tm, d), lambda b, s: (b, s, 0)),
        out_shape=jax.ShapeDtypeStruct((bsz, seq, d), F32),
        compiler_params=_params(("arbitrary", "arbitrary")),
        name="mixer_out_mlp",
    )(x, y, mod, wo, g_row, w1, w2, fg_row)


def _rope(y, cos_t, sin_t):
    half = ROT_DIM // 2
    lane = lax.broadcasted_iota(jnp.int32, cos_t.shape, 1)
    outs = []
    for gidx in range(y.shape[1] // DIFF_DH):
        yh = y[:, gidx * DIFF_DH:(gidx + 1) * DIFF_DH]
        rot = jnp.where(lane < half, pltpu.roll(yh, DIFF_DH - half, 1), pltpu.roll(yh, half, 1))
        outs.append(yh * cos_t + rot * sin_t)
    return outs


def _attn_inproj_kernel(x_ref, mod_ref, kvmod_ref, gq_ref, gkv_ref, w_ref, cos_ref, sin_ref,
                        o_ref, hq_ref, hk_ref):
    j = pl.program_id(2)

    @pl.when(j == 0)
    def _():
        x = x_ref[0]
        xn = x * lax.rsqrt(jnp.mean(x * x, axis=-1, keepdims=True) + EPS)
        m = mod_ref[0]
        km = kvmod_ref[0]
        hq_ref[...] = (xn * (gq_ref[...] * (1.0 + m[1:2])) + m[0:1]).astype(BF16)
        hk_ref[...] = (xn * (gkv_ref[...] * (1.0 + km[1:2])) + km[0:1]).astype(BF16)

    @pl.when(j == 0)
    def _():
        y = _dot(hq_ref[...], w_ref[...])
        q_scale = (DIFF_DH ** -0.5) * LOG2E
        for gidx, yr in enumerate(_rope(y, cos_ref[...], sin_ref[...])):
            o_ref[0, :, gidx * DIFF_DH:(gidx + 1) * DIFF_DH] = (yr * q_scale).astype(BF16)

    @pl.when(j == 1)
    def _():
        y = _dot(hk_ref[...], w_ref[...])
        for gidx, yr in enumerate(_rope(y, cos_ref[...], sin_ref[...])):
            o_ref[0, :, gidx * DIFF_DH:(gidx + 1) * DIFF_DH] = yr.astype(BF16)

    @pl.when(j == 2)
    def _():
        o_ref[0] = _dot(hk_ref[...], w_ref[...]).astype(BF16)


def _attn_inproj(x, mod, kvmod, gq_row, gkv_row, w, cos_t, sin_t, ts=TS_PROJ):
    bsz, seq, d = x.shape
    n_col = w.shape[1] // d
    return pl.pallas_call(
        _attn_inproj_kernel,
        grid=(bsz, seq // ts, n_col),
        in_specs=[
            pl.BlockSpec((1, ts, d), lambda b, s, j: (b, s, 0)),
            pl.BlockSpec((1, 6, d), lambda b, s, j: (b, 0, 0)),
            pl.BlockSpec((1, 2, d), lambda b, s, j: (b, 0, 0)),
            pl.BlockSpec((1, d), lambda b, s, j: (0, 0)),
            pl.BlockSpec((1, d), lambda b, s, j: (0, 0)),
            pl.BlockSpec((d, d), lambda b, s, j: (0, j)),
            pl.BlockSpec((ts, DIFF_DH), lambda b, s, j: (s, 0)),
            pl.BlockSpec((ts, DIFF_DH), lambda b, s, j: (s, 0)),
        ],
        out_specs=pl.BlockSpec((1, ts, d), lambda b, s, j: (b, s, j)),
        out_shape=jax.ShapeDtypeStruct((bsz, seq, n_col * d), BF16),
        scratch_shapes=[pltpu.VMEM((ts, d), BF16), pltpu.VMEM((ts, d), BF16)],
        compiler_params=_params(("arbitrary", "arbitrary", "arbitrary")),
        name="attn_inproj",
    )(x, mod, kvmod, gq_row, gkv_row, w, cos_t, sin_t)


def _diff_attn_kernel(q_ref, k_ref, v_ref, lam_ref, g_ref, o_ref, m_ref, l_ref, acc_ref,
                      s_ref, mx_ref, *, tq, tk, lam_init):
    qi = pl.program_id(2)
    m_ref[...] = jnp.full(m_ref.shape, -jnp.inf, F32)
    l_ref[...] = jnp.zeros_like(l_ref)
    acc_ref[...] = jnp.zeros_like(acc_ref)

    rt = ATTN_ROW_TILE
    n_rt = tq // rt

    def chunks(width):
        return [slice(j * V7X_LANES, (j + 1) * V7X_LANES) for j in range(width // V7X_LANES)]

    def scores_rows(kblk, slot, rq):
        rk = pl.ds(pl.multiple_of(kblk * tk, tk), tk)
        for mi in range(2):
            cs = slice(mi * DIFF_DH, (mi + 1) * DIFF_DH)
            s = _dot_nt(q_ref[0, rq, cs], k_ref[0, rk, cs])
            s_ref[slot, mi, rq, :] = s
            mx_ref[slot, mi, rq, :] = functools.reduce(jnp.maximum, [s[:, c] for c in chunks(tk)])

    def softmax_rows(slot, rq, width, diag_row0):
        if diag_row0 is not None:
            r = lax.broadcasted_iota(jnp.int32, (rt, V7X_LANES), 0)
            c = lax.broadcasted_iota(jnp.int32, (rt, V7X_LANES), 1)
            diff = r - c + diag_row0
        out = []
        for mi in range(2):
            sc = [s_ref[slot, mi, rq, c_] for c_ in chunks(width)]
            if diag_row0 is not None:
                sc = [jnp.where(diff >= c_.start, x, -jnp.inf) for c_, x in zip(chunks(width), sc)]
                mx = functools.reduce(jnp.maximum, sc)
            else:
                mx = mx_ref[slot, mi, rq, :]
            m_old = m_ref[mi, rq, :]
            m_new = jnp.maximum(m_old, jnp.max(mx, axis=-1, keepdims=True))
            alpha = jnp.exp2(m_old - m_new)
            m_ref[mi, rq, :] = m_new
            ps = [jnp.exp2(x - m_new) for x in sc]
            l_ref[mi, rq, :] = alpha * l_ref[mi, rq, :] + functools.reduce(jnp.add, ps)
            out.append((alpha, jnp.concatenate(ps, axis=1).astype(BF16)))
        return out

    def pv_rows(mi, rq, alpha, p, v):
        acc_ref[mi, rq, :] = jnp.concatenate([alpha, alpha], axis=1) * acc_ref[mi, rq, :] + _dot(p, v)

    def row_tile(t):
        return pl.ds(pl.multiple_of(t * rt, rt), rt)

    def first_scores(t, carry):
        scores_rows(0, 0, row_tile(t))
        return carry

    lax.fori_loop(0, n_rt, first_scores, 0)

    blocks_per_k = tk // tq
    n_full = qi // blocks_per_k
    rem = qi - n_full * blocks_per_k

    def visible_block(i, carry):
        slot = i & 1
        v = v_ref[0, pl.ds(pl.multiple_of(i * tk, tk), tk), :]

        def tile(t, c):
            rq = row_tile(t)
            for mi, (alpha, p) in enumerate(softmax_rows(slot, rq, tk, None)):
                pv_rows(mi, rq, alpha, p, v)
            scores_rows(i + 1, 1 - slot, rq)
            return c

        return lax.fori_loop(0, n_rt, tile, carry)

    lax.fori_loop(0, n_full, visible_block, 0)

    last_slot = n_full & 1
    klast = pl.multiple_of(n_full * tk, tk)
    for rem_static in range(blocks_per_k):
        @pl.when(rem == rem_static)
        def _():
            for t in range(n_rt):
                rq = pl.ds(t * rt, rt)
                first_visible = rem_static * tq + t * rt
                width = first_visible + rt
                v = v_ref[0, pl.ds(klast, width), :]
                for mi, (alpha, p) in enumerate(softmax_rows(last_slot, rq, width, first_visible)):
                    pv_rows(mi, rq, alpha, p, v)

    lp = lam_ref[...]
    lam = (jnp.exp(jnp.sum(lp[0:1] * lp[1:2], axis=-1, keepdims=True))
           - jnp.exp(jnp.sum(lp[2:3] * lp[3:4], axis=-1, keepdims=True)) + lam_init)
    l0 = jnp.sum(l_ref[0], axis=-1, keepdims=True)
    l1 = jnp.sum(l_ref[1], axis=-1, keepdims=True)
    o = acc_ref[0] / l0 - lam * (acc_ref[1] / l1)
    on = o * lax.rsqrt(jnp.mean(o * o, axis=-1, keepdims=True) + EPS) * g_ref[...]
    o_ref[0] = (on * (1.0 - lam_init)).astype(BF16)


def _diff_attn(qkv, lam_params, g_row, lam_init, tq=TQ_ATTN, tk=TK_ATTN):
    bsz, seq, _ = qkv.shape
    hw = 2 * DIFF_DH
    assert tk % tq == 0 and seq % tk == 0
    return pl.pallas_call(
        functools.partial(_diff_attn_kernel, tq=tq, tk=tk, lam_init=lam_init),
        grid=(bsz, DIFF_HEADS, seq // tq),
        in_specs=[
            pl.BlockSpec((1, tq, hw), lambda b, h, i: (b, i, h)),
            pl.BlockSpec((1, seq, hw), lambda b, h, i: (b, 0, DIFF_HEADS + h),
                         pipeline_mode=pl.Buffered(1)),
            pl.BlockSpec((1, seq, hw), lambda b, h, i: (b, 0, 2 * DIFF_HEADS + h),
                         pipeline_mode=pl.Buffered(1)),
            pl.BlockSpec((4, DIFF_DH), lambda b, h, i: (0, 0)),
            pl.BlockSpec((1, hw), lambda b, h, i: (0, 0)),
        ],
        out_specs=pl.BlockSpec((1, tq, hw), lambda b, h, i: (b, i, h)),
        out_shape=jax.ShapeDtypeStruct((bsz, seq, DIFF_HEADS * hw), BF16),
        scratch_shapes=[
            pltpu.VMEM((2, tq, V7X_LANES), F32),
            pltpu.VMEM((2, tq, V7X_LANES), F32),
            pltpu.VMEM((2, tq, hw), F32),
            pltpu.VMEM((2, 2, tq, tk), F32),
            pltpu.VMEM((2, 2, tq, V7X_LANES), F32),
        ],
        compiler_params=_params(("arbitrary", "arbitrary", "arbitrary")),
        name="diff_attn",
    )(qkv, qkv, qkv, lam_params, g_row)


def _rope_tables(seq):
    pos = jnp.arange(seq, dtype=F32)
    inv_freq = ROPE_THETA ** (-jnp.arange(0, ROT_DIM, 2, dtype=F32) / ROT_DIM)
    freqs = pos[:, None] * inv_freq[None, :]
    cos, sin = jnp.cos(freqs), jnp.sin(freqs)
    pad = DIFF_DH - ROT_DIM
    cos_t = jnp.concatenate([cos, cos, jnp.ones((seq, pad), F32)], axis=1)
    sin_t = jnp.concatenate([-sin, sin, jnp.zeros((seq, pad), F32)], axis=1)
    return cos_t, sin_t


def kernel(x, c, mod_w, mod_b, norm_mix_g, norm_mlp_g, a_w_in, a_conv_w, a_log, a_dt_bias, a_out_norm_g, a_w_out,
           kv_mod_w, kv_mod_b, kv_norm_g, kv_w, b_w_q, b_lambda, b_subln_g, b_w_out, mlp_w1, mlp_w2, final_g):
    bsz, seq, d = x.shape
    depth = mod_w.shape[0]
    assert depth == 2 and a_w_in.shape[0] == 1 and b_w_q.shape[0] == 1
    assert seq % TS_PROJ == 0 and seq % TS_SCAN == 0 and seq % TM_MLP == 0 and seq % TQ_ATTN == 0

    c_t = c.T
    mod = _mod_call(c_t, mod_w, mod_b[:, None, :]).reshape(depth, bsz, 6, d)
    kvmod = _mod_call(c_t, kv_mod_w[None], kv_mod_b[None, None, :]).reshape(bsz, 2, d)

    row = lambda v: v.reshape(1, -1).astype(F32)

    w_in = a_w_in[0]
    n_main = 4 * d
    w_main = w_in[:, :n_main].astype(BF16)
    w_ab = jnp.pad(w_in[:, n_main:], ((0, 0), (0, V7X_LANES - 2 * GDN_HEADS))).astype(BF16)
    ad = jnp.pad(jnp.stack([a_log[0], a_dt_bias[0]]).astype(F32), ((0, 0), (0, V7X_LANES - GDN_HEADS)))
    qkvz, gb = _gdn_inproj(x, mod[0], row(norm_mix_g[0]), w_main, w_ab, a_conv_w[0].astype(F32), ad)
    y0 = _gdn_scan(qkvz, gb, row(a_out_norm_g[0]))
    x = _mixer_out_mlp(x, y0, mod[0], a_w_out[0].astype(BF16), row(norm_mlp_g[0]),
                       mlp_w1[0].astype(BF16), mlp_w2[0].astype(BF16), row(final_g), False)

    lam_init = 0.8 - 0.6 * math.exp(-0.3 * 1)
    w_qkv = jnp.concatenate([b_w_q[0], kv_w], axis=1).astype(BF16)
    cos_t, sin_t = _rope_tables(seq)
    qkv = _attn_inproj(x, mod[1], kvmod, row(norm_mix_g[1]), row(kv_norm_g), w_qkv, cos_t, sin_t)
    y1 = _diff_attn(qkv, b_lambda[0].astype(F32), row(b_subln_g[0]), lam_init)
    x = _mixer_out_mlp(x, y1, mod[1], b_w_out[0].astype(BF16), row(norm_mlp_g[1]),
                       mlp_w1[1].astype(BF16), mlp_w2[1].astype(BF16), row(final_g), True)
    return x
```

```python
import functools
import math

import jax
import jax.numpy as jnp
from jax import lax
from jax.experimental import pallas as pl
from jax.experimental.pallas import tpu as pltpu

F32 = jnp.float32
BF16 = jnp.bfloat16
HIGHEST = lax.Precision.HIGHEST

V7X_LANES = 128
V7X_SUBLANES = 8
V7X_VMEM_LIMIT_BYTES = 56 * 1024 * 1024

EPS = 1e-6
GDN_HEADS = 8
GDN_DK = 128
GDN_CHUNK = 64
GDN_CHUNKS_PER_STEP = 2
CONV_K = 4
DIFF_HEADS = 4
DIFF_DH = 128
ROT_DIM = DIFF_DH // 4
ROPE_THETA = 500000.0
LOG2E = 1.4426950408889634

TS_PROJ = 512
TS_SCAN = 512
TM_MLP = 512
TQ_ATTN = 512
TK_ATTN = 2048
ATTN_ROW_TILE = 256
FF_CHUNK = 1024


def _params(sem, vmem=V7X_VMEM_LIMIT_BYTES, flags=None):
    return pltpu.CompilerParams(dimension_semantics=sem, vmem_limit_bytes=vmem, flags=flags)


def _sigmoid(x):
    return 1.0 / (1.0 + jnp.exp(-x))


def _silu(x):
    return x * _sigmoid(x)


def _softplus(x):
    return jnp.maximum(x, 0.0) + jnp.log1p(jnp.exp(-jnp.abs(x)))


def _dot(a, b, precision=None):
    return jnp.dot(a, b, preferred_element_type=F32, precision=precision)


def _dot_nt(a, b):
    return lax.dot_general(a, b, (((1,), (1,)), ((), ())), preferred_element_type=F32)


def _dot_tn(a, b):
    return lax.dot_general(a, b, (((0,), (0,)), ((), ())), preferred_element_type=F32)


def _split_bf16(x):
    hi = x.astype(BF16)
    return hi, (x - hi.astype(F32)).astype(BF16)


def _dot3(a, b):
    a_hi, a_lo = _split_bf16(a)
    b_hi, b_lo = _split_bf16(b)
    return _dot(jnp.concatenate([a_hi, a_lo, a_hi], axis=1),
                jnp.concatenate([b_hi, b_hi, b_lo], axis=0))


def _mod_kernel(ct_ref, w_ref, b_ref, o_ref, *, batch):
    cs = _silu(ct_ref[...])
    w = w_ref[0]
    for b in range(batch):
        o_ref[0, b:b + 1, :] = jnp.sum(w * cs[:, b:b + 1], axis=0, keepdims=True) + b_ref[0]


def _mod_call(c_t, w, bias, tn=2048):
    n_l, d, n = w.shape
    batch = c_t.shape[1]
    return pl.pallas_call(
        functools.partial(_mod_kernel, batch=batch),
        grid=(n_l, n // tn),
        in_specs=[
            pl.BlockSpec((d, batch), lambda l, j: (0, 0)),
            pl.BlockSpec((1, d, tn), lambda l, j: (l, 0, j)),
            pl.BlockSpec((1, 1, tn), lambda l, j: (l, 0, j)),
        ],
        out_specs=pl.BlockSpec((1, batch, tn), lambda l, j: (l, 0, j)),
        out_shape=jax.ShapeDtypeStruct((n_l, batch, n), F32),
        compiler_params=_params(("arbitrary", "arbitrary")),
        name="mod_vectors",
    )(c_t, w, bias)


def _modulated_norm(x, g_row, shift_row, scale_row):
    rstd = lax.rsqrt(jnp.mean(x * x, axis=-1, keepdims=True) + EPS)
    return (x * rstd) * (g_row * (1.0 + scale_row)) + shift_row


def _gdn_inproj_kernel(x_ref, mod_ref, g_ref, w_ref, wab_ref, cw_ref, ad_ref,
                       o_ref, gb_ref, h_ref, big_ref, carry_ref, *, ts):
    s = pl.program_id(1)
    j = pl.program_id(2)

    @pl.when(j == 0)
    def _():
        m = mod_ref[0]
        h = _modulated_norm(x_ref[0], g_ref[...], m[0:1], m[1:2])
        hb = h.astype(BF16)
        h_ref[...] = hb
        ab = _dot(hb, wab_ref[...])
        ad = ad_ref[...]
        g = -jnp.exp(ad[0:1]) * _softplus(ab + ad[1:2])
        lane = lax.broadcasted_iota(jnp.int32, ab.shape, 1)
        gb_ref[0] = jnp.where(lane < GDN_HEADS, g, _sigmoid(ab))

    @pl.when(j == 3)
    def _():
        o_ref[0] = _dot(h_ref[...], w_ref[...])

    @pl.when(j < 3)
    def _():
        proj = _dot(h_ref[...], w_ref[...])
        jj = jnp.minimum(j, 2)
        prev = carry_ref[jj]
        big_ref[0:V7X_SUBLANES] = jnp.where(s == 0, jnp.zeros_like(prev), prev)
        big_ref[V7X_SUBLANES:] = proj
        carry_ref[jj] = proj[ts - V7X_SUBLANES:]
        cw = cw_ref[...]
        base = V7X_SUBLANES - (CONV_K - 1)
        y = big_ref[base:base + ts] * cw[0:1]
        for t in range(1, CONV_K):
            y = y + big_ref[base + t:base + t + ts] * cw[t:t + 1]
        y = _silu(y)
        is_qk = j < 2
        for hh in range(GDN_HEADS):
            cs = slice(hh * GDN_DK, (hh + 1) * GDN_DK)
            yh = y[:, cs]
            yn = yh * lax.rsqrt(jnp.sum(yh * yh, axis=-1, keepdims=True) + EPS)
            o_ref[0, :, cs] = jnp.where(is_qk, yn, yh)


def _gdn_inproj(x, mod, g_row, w, wab, cw, ad, ts=TS_PROJ):
    bsz, seq, d = x.shape
    n_col = w.shape[1] // d
    return pl.pallas_call(
        functools.partial(_gdn_inproj_kernel, ts=ts),
        grid=(bsz, seq // ts, n_col),
        in_specs=[
            pl.BlockSpec((1, ts, d), lambda b, s, j: (b, s, 0)),
            pl.BlockSpec((1, 6, d), lambda b, s, j: (b, 0, 0)),
            pl.BlockSpec((1, d), lambda b, s, j: (0, 0)),
            pl.BlockSpec((d, d), lambda b, s, j: (0, j)),
            pl.BlockSpec((d, V7X_LANES), lambda b, s, j: (0, 0)),
            pl.BlockSpec((CONV_K, d), lambda b, s, j: (0, jnp.minimum(j, 2))),
            pl.BlockSpec((2, V7X_LANES), lambda b, s, j: (0, 0)),
        ],
        out_specs=[
            pl.BlockSpec((1, ts, d), lambda b, s, j: (b, s, j)),
            pl.BlockSpec((1, ts, V7X_LANES), lambda b, s, j: (b, s, 0)),
        ],
        out_shape=[
            jax.ShapeDtypeStruct((bsz, seq, n_col * d), F32),
            jax.ShapeDtypeStruct((bsz, seq, V7X_LANES), F32),
        ],
        scratch_shapes=[
            pltpu.VMEM((ts, d), BF16),
            pltpu.VMEM((ts + V7X_SUBLANES, d), F32),
            pltpu.VMEM((3, V7X_SUBLANES, d), F32),
        ],
        compiler_params=_params(("arbitrary", "arbitrary", "arbitrary")),
        name="gdn_inproj",
    )(x, mod, g_row, w, wab, cw, ad)


def _gdn_scan_kernel(q_ref, k_ref, v_ref, z_ref, gb_ref, og_ref, y_ref, state_ref, *, ts):
    c_sz = GDN_CHUNK

    @pl.when(pl.program_id(1) == 0)
    def _():
        state_ref[...] = jnp.zeros_like(state_ref)

    row = lax.broadcasted_iota(jnp.int32, (c_sz, c_sz), 0)
    col = lax.broadcasted_iota(jnp.int32, (c_sz, c_sz), 1)
    incl = row >= col
    strict = row > col
    ltri = incl.astype(F32)
    eye = (row == col).astype(F32)
    og = og_ref[...]
    q_scale = GDN_DK ** -0.5

    heads = range(GDN_HEADS)
    cols = [slice(h * GDN_DK, (h + 1) * GDN_DK) for h in heads]
    n_sub = GDN_CHUNKS_PER_STEP

    def step(c, carry):
        rows, gbc, gcum, gcum_t = [], [], [], []
        for ci in range(n_sub):
            r0 = pl.multiple_of((c * n_sub + ci) * c_sz, c_sz)
            rows.append(pl.ds(r0, c_sz))
            gbc.append(gb_ref[0, rows[ci], :])
            gcum.append(_dot(ltri, gbc[ci], HIGHEST))
            gcum_t.append(gcum[ci].T)
        streams = [(ci, h) for ci in range(n_sub) for h in heads]
        qh = {s: q_ref[0, rows[s[0]], cols[s[1]]] * q_scale for s in streams}
        kh = {s: k_ref[0, rows[s[0]], cols[s[1]]] for s in streams}
        gc = {(ci, h): gcum[ci][:, h:h + 1] for ci, h in streams}
        beta = {(ci, h): gbc[ci][:, GDN_HEADS + h:GDN_HEADS + h + 1] for ci, h in streams}
        glast = {(ci, h): gcum[ci][c_sz - 1:c_sz, h:h + 1] for ci, h in streams}
        eg = {s: jnp.exp(gc[s]) for s in streams}
        decay = {(ci, h): jnp.where(incl, jnp.exp(jnp.where(incl, gc[(ci, h)] - gcum_t[ci][h:h + 1, :],
                                                           0.0)), 0.0) for ci, h in streams}
        kb = {s: kh[s] * beta[s] for s in streams}
        kk = {s: _dot_nt(jnp.concatenate([kb[s], qh[s]], axis=0).astype(BF16), kh[s].astype(BF16))
              for s in streams}
        a_mat = {s: jnp.where(strict, kk[s][:c_sz] * decay[s], 0.0) for s in streams}
        qk = {s: (kk[s][c_sz:] * decay[s]).astype(BF16) for s in streams}
        x_mat = {s: eye - a_mat[s] for s in streams}
        p_mat = {s: _dot3(a_mat[s], a_mat[s]) for s in streams}
        for _ in range(4):
            xp = {s: _dot3(jnp.concatenate([x_mat[s], p_mat[s]], axis=0), p_mat[s]) for s in streams}
            x_mat = {s: x_mat[s] + xp[s][:c_sz] for s in streams}
            p_mat = {s: xp[s][c_sz:] for s in streams}
        xl = {s: _dot3(x_mat[s], p_mat[s]) for s in streams}
        x_mat = {s: x_mat[s] + xl[s] for s in streams}
        sol = {s: _dot3(x_mat[s], jnp.concatenate([v_ref[0, rows[s[0]], cols[s[1]]] * beta[s],
                                                   kb[s] * eg[s]], axis=1))
               for s in streams}
        wq_lhs = {s: jnp.concatenate([sol[s][:, GDN_DK:], qh[s] * eg[s]], axis=0).astype(BF16)
                  for s in streams}
        kdec = {s: (kh[s] * jnp.exp(glast[s] - gc[s])).astype(BF16) for s in streams}
        for ci in range(n_sub):
            st = [state_ref[h] for h in heads]
            wq = [_dot(wq_lhs[(ci, h)], st[h].astype(BF16)) for h in heads]
            v_new = [(sol[(ci, h)][:, :GDN_DK] - wq[h][:c_sz]).astype(BF16) for h in heads]
            o = [wq[h][c_sz:] + _dot(qk[(ci, h)], v_new[h]) for h in heads]
            for h in heads:
                state_ref[h] = st[h] * jnp.exp(glast[(ci, h)]) + _dot_tn(kdec[(ci, h)], v_new[h])
            for h in heads:
                on = o[h] * lax.rsqrt(jnp.mean(o[h] * o[h], axis=-1, keepdims=True) + EPS) * og
                y_ref[0, rows[ci], cols[h]] = (on * _silu(z_ref[0, rows[ci], cols[h]])).astype(BF16)
        return carry

    lax.fori_loop(0, ts // (c_sz * n_sub), step, 0)


def _gdn_scan(qkvz, gb, og_row, ts=TS_SCAN):
    bsz, seq, _ = qkvz.shape
    d = GDN_HEADS * GDN_DK
    col = lambda j: pl.BlockSpec((1, ts, d), lambda b, s: (b, s, j))
    return pl.pallas_call(
        functools.partial(_gdn_scan_kernel, ts=ts),
        grid=(bsz, seq // ts),
        in_specs=[col(0), col(1), col(2), col(3),
                  pl.BlockSpec((1, ts, V7X_LANES), lambda b, s: (b, s, 0)),
                  pl.BlockSpec((1, GDN_DK), lambda b, s: (0, 0))],
        out_specs=pl.BlockSpec((1, ts, d), lambda b, s: (b, s, 0)),
        out_shape=jax.ShapeDtypeStruct((bsz, seq, d), BF16),
        scratch_shapes=[pltpu.VMEM((GDN_HEADS, GDN_DK, GDN_DK), F32)],
        compiler_params=_params(("arbitrary", "arbitrary")),
        name="gdn_scan",
    )(qkvz, qkvz, qkvz, qkvz, gb, og_row)


def _mixer_out_mlp_kernel(x_ref, y_ref, mod_ref, wo_ref, g_ref, w1_ref, w2_ref, fg_ref, o_ref,
                          *, final_norm):
    m = mod_ref[0]
    x1 = x_ref[0] + m[2:3] * _dot(y_ref[0], wo_ref[...])
    h = _modulated_norm(x1, g_ref[...], m[3:4], m[4:5]).astype(BF16)
    d_ff = w1_ref.shape[1]
    acc = None
    for c in range(d_ff // FF_CHUNK):
        cs = slice(c * FF_CHUNK, (c + 1) * FF_CHUNK)
        hid = jnp.maximum(_dot(h, w1_ref[:, cs]), 0.0)
        part = _dot((hid * hid).astype(BF16), w2_ref[cs, :])
        acc = part if acc is None else acc + part
    x2 = x1 + m[5:6] * acc
    if final_norm:
        x2 = x2 * lax.rsqrt(jnp.mean(x2 * x2, axis=-1, keepdims=True) + EPS) * fg_ref[...]
    o_ref[0] = x2


def _mixer_out_mlp(x, y, mod, wo, g_row, w1, w2, fg_row, final_norm, tm=TM_MLP):
    bsz, seq, d = x.shape
    d_ff = w1.shape[1]
    const = lambda shape: pl.BlockSpec(shape, lambda b, s: (0,) * len(shape),
                                       pipeline_mode=pl.Buffered(1))
    return pl.pallas_call(
        functools.partial(_mixer_out_mlp_kernel, final_norm=final_norm),
        grid=(bsz, seq // tm),
        in_specs=[
            pl.BlockSpec((1, tm, d), lambda b, s: (b, s, 0)),
            pl.BlockSpec((1, tm, d), lambda b, s: (b, s, 0)),
            pl.BlockSpec((1, 6, d), lambda b, s: (b, 0, 0)),
            const((d, d)),
            const((1, d)),
            const((d, d_ff)),
            const((d_ff, d)),
            const((1, d)),
        ],
        out_specs=pl.BlockSpec((1, tm, d), lambda b, s: (b, s, 0)),
        out_shape=jax.ShapeDtypeStruct((bsz, seq, d), F32),
        compiler_params=_params(("arbitrary", "arbitrary")),
        name="mixer_out_mlp",
    )(x, y, mod, wo, g_row, w1, w2, fg_row)


def _rope_lane_order():
    half = ROT_DIM // 2
    mid = DIFF_DH // 2
    return (list(range(half)) + list(range(ROT_DIM, mid + half)) + list(range(half, ROT_DIM))
            + list(range(mid + half, DIFF_DH)))


def _rope(y, cos_t, sin_t):
    outs = []
    for gidx in range(y.shape[1] // DIFF_DH):
        yh = y[:, gidx * DIFF_DH:(gidx + 1) * DIFF_DH]
        outs.append(yh * cos_t + pltpu.roll(yh, DIFF_DH // 2, 1) * sin_t)
    return outs


def _attn_inproj_kernel(x_ref, mod_ref, kvmod_ref, gq_ref, gkv_ref, w_ref, cos_ref, sin_ref,
                        o_ref, hq_ref, hk_ref):
    j = pl.program_id(2)

    @pl.when(j == 0)
    def _():
        x = x_ref[0]
        xn = x * lax.rsqrt(jnp.mean(x * x, axis=-1, keepdims=True) + EPS)
        m = mod_ref[0]
        km = kvmod_ref[0]
        hq_ref[...] = (xn * (gq_ref[...] * (1.0 + m[1:2])) + m[0:1]).astype(BF16)
        hk_ref[...] = (xn * (gkv_ref[...] * (1.0 + km[1:2])) + km[0:1]).astype(BF16)

    @pl.when(j == 0)
    def _():
        y = _dot(hq_ref[...], w_ref[...])
        q_scale = (DIFF_DH ** -0.5) * LOG2E
        for gidx, yr in enumerate(_rope(y, cos_ref[...], sin_ref[...])):
            o_ref[0, :, gidx * DIFF_DH:(gidx + 1) * DIFF_DH] = (yr * q_scale).astype(BF16)

    @pl.when(j == 1)
    def _():
        y = _dot(hk_ref[...], w_ref[...])
        for gidx, yr in enumerate(_rope(y, cos_ref[...], sin_ref[...])):
            o_ref[0, :, gidx * DIFF_DH:(gidx + 1) * DIFF_DH] = yr.astype(BF16)

    @pl.when(j == 2)
    def _():
        o_ref[0] = _dot(hk_ref[...], w_ref[...]).astype(BF16)


def _attn_inproj(x, mod, kvmod, gq_row, gkv_row, w, cos_t, sin_t, ts=TS_PROJ):
    bsz, seq, d = x.shape
    n_col = w.shape[1] // d
    return pl.pallas_call(
        _attn_inproj_kernel,
        grid=(bsz, seq // ts, n_col),
        in_specs=[
            pl.BlockSpec((1, ts, d), lambda b, s, j: (b, s, 0)),
            pl.BlockSpec((1, 6, d), lambda b, s, j: (b, 0, 0)),
            pl.BlockSpec((1, 2, d), lambda b, s, j: (b, 0, 0)),
            pl.BlockSpec((1, d), lambda b, s, j: (0, 0)),
            pl.BlockSpec((1, d), lambda b, s, j: (0, 0)),
            pl.BlockSpec((d, d), lambda b, s, j: (0, j)),
            pl.BlockSpec((ts, DIFF_DH), lambda b, s, j: (s, 0)),
            pl.BlockSpec((ts, DIFF_DH), lambda b, s, j: (s, 0)),
        ],
        out_specs=pl.BlockSpec((1, ts, d), lambda b, s, j: (b, s, j)),
        out_shape=jax.ShapeDtypeStruct((bsz, seq, n_col * d), BF16),
        scratch_shapes=[pltpu.VMEM((ts, d), BF16), pltpu.VMEM((ts, d), BF16)],
        compiler_params=_params(("arbitrary", "arbitrary", "arbitrary")),
        name="attn_inproj",
    )(x, mod, kvmod, gq_row, gkv_row, w, cos_t, sin_t)


def _diff_attn_kernel(q_ref, k_ref, v_ref, lam_ref, g_ref, o_ref, m_ref, l_ref, acc_ref,
                      s_ref, mx_ref, *, tq, tk, lam_init):
    qi = pl.program_id(2)
    m_ref[...] = jnp.full(m_ref.shape, -jnp.inf, F32)
    l_ref[...] = jnp.zeros_like(l_ref)
    acc_ref[...] = jnp.zeros_like(acc_ref)

    rt = ATTN_ROW_TILE
    n_rt = tq // rt

    def chunks(width):
        return [slice(j * V7X_LANES, (j + 1) * V7X_LANES) for j in range(width // V7X_LANES)]

    def scores_rows(kblk, slot, rq, width=None):
        rk = pl.ds(pl.multiple_of(kblk * tk, tk), tk if width is None else width)
        for mi in range(2):
            cs = slice(mi * DIFF_DH, (mi + 1) * DIFF_DH)
            s = _dot_nt(q_ref[0, rq, cs], k_ref[0, rk, cs])
            if width is None:
                s_ref[slot, mi, rq, :] = s
                mx_ref[slot, mi, rq, :] = functools.reduce(jnp.maximum, [s[:, c] for c in chunks(tk)])
            else:
                s_ref[slot, mi, rq, 0:width] = s

    def softmax_rows(slot, rq, width, diag_row0):
        if diag_row0 is not None:
            r = lax.broadcasted_iota(jnp.int32, (rt, V7X_LANES), 0)
            c = lax.broadcasted_iota(jnp.int32, (rt, V7X_LANES), 1)
            diff = r - c + diag_row0
        out = []
        for mi in range(2):
            sc = [s_ref[slot, mi, rq, c_] for c_ in chunks(width)]
            if diag_row0 is not None:
                sc = [jnp.where(diff >= c_.start, x, -jnp.inf) for c_, x in zip(chunks(width), sc)]
                mx = functools.reduce(jnp.maximum, sc)
            else:
                mx = mx_ref[slot, mi, rq, :]
            m_old = m_ref[mi, rq, :]
            m_new = jnp.maximum(m_old, jnp.max(mx, axis=-1, keepdims=True))
            alpha = jnp.exp2(m_old - m_new)
            m_ref[mi, rq, :] = m_new
            ps = [jnp.exp2(x - m_new) for x in sc]
            l_ref[mi, rq, :] = alpha * l_ref[mi, rq, :] + functools.reduce(jnp.add, ps)
            out.append((alpha, jnp.concatenate(ps, axis=1).astype(BF16)))
        return out

    def pv_rows(mi, rq, alpha, p, v):
        acc_ref[mi, rq, :] = jnp.concatenate([alpha, alpha], axis=1) * acc_ref[mi, rq, :] + _dot(p, v)

    def row_tile(t):
        return pl.ds(pl.multiple_of(t * rt, rt), rt)

    blocks_per_k = tk // tq
    n_full = qi // blocks_per_k
    rem = qi - n_full * blocks_per_k

    kdiag = pl.multiple_of(n_full * tk, tk)
    for rem_static in range(blocks_per_k):
        @pl.when(rem == rem_static)
        def _():
            for t in range(n_rt):
                scores_rows(n_full, 0, pl.ds(t * rt, rt), width=rem_static * tq + (t + 1) * rt)
            for t in range(n_rt):
                rq = pl.ds(t * rt, rt)
                first_visible = rem_static * tq + t * rt
                width = first_visible + rt
                v = v_ref[0, pl.ds(kdiag, width), :]
                for mi, (alpha, p) in enumerate(softmax_rows(0, rq, width, first_visible)):
                    pv_rows(mi, rq, alpha, p, v)
                scores_rows(0, 1, rq)

    def visible_block(i, produce_next):
        slot = (i + 1) & 1
        v = v_ref[0, pl.ds(pl.multiple_of(i * tk, tk), tk), :]

        def tile(t, c):
            rq = row_tile(t)
            for mi, (alpha, p) in enumerate(softmax_rows(slot, rq, tk, None)):
                pv_rows(mi, rq, alpha, p, v)
            if produce_next:
                scores_rows(i + 1, 1 - slot, rq)
            return c

        lax.fori_loop(0, n_rt, tile, 0)

    def visible_step(i, carry):
        visible_block(i, True)
        return carry

    lax.fori_loop(0, n_full - 1, visible_step, 0)

    @pl.when(n_full > 0)
    def _():
        visible_block(n_full - 1, False)

    lp = lam_ref[...]
    lam = (jnp.exp(jnp.sum(lp[0:1] * lp[1:2], axis=-1, keepdims=True))
           - jnp.exp(jnp.sum(lp[2:3] * lp[3:4], axis=-1, keepdims=True)) + lam_init)
    l0 = jnp.sum(l_ref[0], axis=-1, keepdims=True)
    l1 = jnp.sum(l_ref[1], axis=-1, keepdims=True)
    o = acc_ref[0] / l0 - lam * (acc_ref[1] / l1)
    on = o * lax.rsqrt(jnp.mean(o * o, axis=-1, keepdims=True) + EPS) * g_ref[...]
    o_ref[0] = (on * (1.0 - lam_init)).astype(BF16)


def _diff_attn(qkv, lam_params, g_row, lam_init, tq=TQ_ATTN, tk=TK_ATTN):
    bsz, seq, _ = qkv.shape
    hw = 2 * DIFF_DH
    assert tk % tq == 0 and seq % tk == 0
    return pl.pallas_call(
        functools.partial(_diff_attn_kernel, tq=tq, tk=tk, lam_init=lam_init),
        grid=(bsz, DIFF_HEADS, seq // tq),
        in_specs=[
            pl.BlockSpec((1, tq, hw), lambda b, h, i: (b, i, h)),
            pl.BlockSpec((1, seq, hw), lambda b, h, i: (b, 0, DIFF_HEADS + h),
                         pipeline_mode=pl.Buffered(1)),
            pl.BlockSpec((1, seq, hw), lambda b, h, i: (b, 0, 2 * DIFF_HEADS + h),
                         pipeline_mode=pl.Buffered(1)),
            pl.BlockSpec((4, DIFF_DH), lambda b, h, i: (0, 0)),
            pl.BlockSpec((1, hw), lambda b, h, i: (0, 0)),
        ],
        out_specs=pl.BlockSpec((1, tq, hw), lambda b, h, i: (b, i, h)),
        out_shape=jax.ShapeDtypeStruct((bsz, seq, DIFF_HEADS * hw), BF16),
        scratch_shapes=[
            pltpu.VMEM((2, tq, V7X_LANES), F32),
            pltpu.VMEM((2, tq, V7X_LANES), F32),
            pltpu.VMEM((2, tq, hw), F32),
            pltpu.VMEM((2, 2, tq, tk), F32),
            pltpu.VMEM((2, 2, tq, V7X_LANES), F32),
        ],
        compiler_params=_params(("arbitrary", "arbitrary", "arbitrary")),
        name="diff_attn",
    )(qkv, qkv, qkv, lam_params, g_row)


def _rope_tables(seq):
    pos = jnp.arange(seq, dtype=F32)
    inv_freq = ROPE_THETA ** (-jnp.arange(0, ROT_DIM, 2, dtype=F32) / ROT_DIM)
    freqs = pos[:, None] * inv_freq[None, :]
    cos, sin = jnp.cos(freqs), jnp.sin(freqs)
    pad = (DIFF_DH - ROT_DIM) // 2
    ones, zeros = jnp.ones((seq, pad), F32), jnp.zeros((seq, pad), F32)
    cos_t = jnp.concatenate([cos, ones, cos, ones], axis=1)
    sin_t = jnp.concatenate([-sin, zeros, sin, zeros], axis=1)
    return cos_t, sin_t


def kernel(x, c, mod_w, mod_b, norm_mix_g, norm_mlp_g, a_w_in, a_conv_w, a_log, a_dt_bias, a_out_norm_g, a_w_out,
           kv_mod_w, kv_mod_b, kv_norm_g, kv_w, b_w_q, b_lambda, b_subln_g, b_w_out, mlp_w1, mlp_w2, final_g):
    bsz, seq, d = x.shape
    depth = mod_w.shape[0]
    assert depth == 2 and a_w_in.shape[0] == 1 and b_w_q.shape[0] == 1
    assert seq % TS_PROJ == 0 and seq % TS_SCAN == 0 and seq % TM_MLP == 0 and seq % TQ_ATTN == 0

    c_t = c.T
    mod = _mod_call(c_t, mod_w, mod_b[:, None, :]).reshape(depth, bsz, 6, d)
    kvmod = _mod_call(c_t, kv_mod_w[None], kv_mod_b[None, None, :]).reshape(bsz, 2, d)

    row = lambda v: v.reshape(1, -1).astype(F32)

    w_in = a_w_in[0]
    n_main = 4 * d
    w_main = w_in[:, :n_main].astype(BF16)
    w_ab = jnp.pad(w_in[:, n_main:], ((0, 0), (0, V7X_LANES - 2 * GDN_HEADS))).astype(BF16)
    ad = jnp.pad(jnp.stack([a_log[0], a_dt_bias[0]]).astype(F32), ((0, 0), (0, V7X_LANES - GDN_HEADS)))
    qkvz, gb = _gdn_inproj(x, mod[0], row(norm_mix_g[0]), w_main, w_ab, a_conv_w[0].astype(F32), ad)
    y0 = _gdn_scan(qkvz, gb, row(a_out_norm_g[0]))
    x = _mixer_out_mlp(x, y0, mod[0], a_w_out[0].astype(BF16), row(norm_mlp_g[0]),
                       mlp_w1[0].astype(BF16), mlp_w2[0].astype(BF16), row(final_g), False)

    lam_init = 0.8 - 0.6 * math.exp(-0.3 * 1)
    n_qk_groups = 2 * DIFF_HEADS * 2
    order = jnp.asarray([g * DIFF_DH + i for g in range(n_qk_groups) for i in _rope_lane_order()]
                        + list(range(n_qk_groups * DIFF_DH, 3 * d)), jnp.int32)
    w_qkv = jnp.concatenate([b_w_q[0], kv_w], axis=1)[:, order].astype(BF16)
    cos_t, sin_t = _rope_tables(seq)
    qkv = _attn_inproj(x, mod[1], kvmod, row(norm_mix_g[1]), row(kv_norm_g), w_qkv, cos_t, sin_t)
    y1 = _diff_attn(qkv, b_lambda[0].astype(F32), row(b_subln_g[0]), lam_init)
    x = _mixer_out_mlp(x, y1, mod[1], b_w_out[0].astype(BF16), row(norm_mlp_g[1]),
                       mlp_w1[1].astype(BF16), mlp_w2[1].astype(BF16), row(final_g), True)
    return x
```

```python
import functools
import math

import jax
import jax.numpy as jnp
from jax import lax
from jax.experimental import pallas as pl
from jax.experimental.pallas import tpu as pltpu

F32 = jnp.float32
BF16 = jnp.bfloat16
HIGHEST = lax.Precision.HIGHEST

V7X_LANES = 128
V7X_SUBLANES = 8
V7X_VMEM_LIMIT_BYTES = 56 * 1024 * 1024

EPS = 1e-6
GDN_HEADS = 8
GDN_DK = 128
GDN_CHUNK = 64
GDN_CHUNKS_PER_STEP = 2
CONV_K = 4
DIFF_HEADS = 4
DIFF_DH = 128
ROT_DIM = DIFF_DH // 4
ROPE_THETA = 500000.0
LOG2E = 1.4426950408889634

TS_PROJ = 512
TS_SCAN = 1024
TM_MLP = 512
TQ_ATTN = 512
TK_ATTN = 2048
ATTN_ROW_TILE = 256
FF_CHUNK = 1024


def _params(sem, vmem=V7X_VMEM_LIMIT_BYTES, flags=None):
    return pltpu.CompilerParams(dimension_semantics=sem, vmem_limit_bytes=vmem, flags=flags)


def _sigmoid(x):
    return 1.0 / (1.0 + jnp.exp(-x))


def _silu(x):
    return x * _sigmoid(x)


def _softplus(x):
    return jnp.maximum(x, 0.0) + jnp.log1p(jnp.exp(-jnp.abs(x)))


def _dot(a, b, precision=None):
    return jnp.dot(a, b, preferred_element_type=F32, precision=precision)


def _dot_nt(a, b):
    return lax.dot_general(a, b, (((1,), (1,)), ((), ())), preferred_element_type=F32)


def _dot_tn(a, b):
    return lax.dot_general(a, b, (((0,), (0,)), ((), ())), preferred_element_type=F32)


def _dot3_dup(a_dup, b):
    k = b.shape[0]
    a_hi = a_dup.astype(BF16)
    a_hi32 = a_hi.astype(F32)
    low = lax.broadcasted_iota(jnp.int32, a_dup.shape, 1) < k
    hi_lo = jnp.where(low, a_hi32, a_dup - a_hi32).astype(BF16)
    b_hi = b.astype(BF16)
    b_lo = (b - b_hi.astype(F32)).astype(BF16)
    rhs = jnp.concatenate([b_hi, b_hi, b_lo, jnp.zeros_like(b_hi)], axis=0)
    return _dot(jnp.concatenate([hi_lo, a_hi], axis=1), rhs)


def _mod_kernel(ct_ref, w_ref, b_ref, o_ref, *, batch):
    cs = _silu(ct_ref[...])
    w = w_ref[0]
    for b in range(batch):
        o_ref[0, b:b + 1, :] = jnp.sum(w * cs[:, b:b + 1], axis=0, keepdims=True) + b_ref[0]


def _mod_call(c_t, w, bias, tn=2048):
    n_l, d, n = w.shape
    batch = c_t.shape[1]
    return pl.pallas_call(
        functools.partial(_mod_kernel, batch=batch),
        grid=(n_l, n // tn),
        in_specs=[
            pl.BlockSpec((d, batch), lambda l, j: (0, 0)),
            pl.BlockSpec((1, d, tn), lambda l, j: (l, 0, j)),
            pl.BlockSpec((1, 1, tn), lambda l, j: (l, 0, j)),
        ],
        out_specs=pl.BlockSpec((1, batch, tn), lambda l, j: (l, 0, j)),
        out_shape=jax.ShapeDtypeStruct((n_l, batch, n), F32),
        compiler_params=_params(("arbitrary", "arbitrary")),
        name="mod_vectors",
    )(c_t, w, bias)


def _modulated_norm(x, g_row, shift_row, scale_row):
    rstd = lax.rsqrt(jnp.mean(x * x, axis=-1, keepdims=True) + EPS)
    return (x * rstd) * (g_row * (1.0 + scale_row)) + shift_row


def _gdn_inproj_kernel(x_ref, mod_ref, g_ref, w_ref, wab_ref, cw_ref, ad_ref,
                       o_ref, gb_ref, h_ref, big_ref, carry_ref, *, ts):
    s = pl.program_id(1)
    j = pl.program_id(2)

    @pl.when(j == 0)
    def _():
        m = mod_ref[0]
        h = _modulated_norm(x_ref[0], g_ref[...], m[0:1], m[1:2])
        hb = h.astype(BF16)
        h_ref[...] = hb
        ab = _dot(hb, wab_ref[...])
        ad = ad_ref[...]
        g = -jnp.exp(ad[0:1]) * _softplus(ab + ad[1:2])
        lane = lax.broadcasted_iota(jnp.int32, ab.shape, 1)
        gb_ref[0] = jnp.where(lane < GDN_HEADS, g, _sigmoid(ab))

    @pl.when(j == 3)
    def _():
        o_ref[0] = _dot(h_ref[...], w_ref[...])

    @pl.when(j < 3)
    def _():
        proj = _dot(h_ref[...], w_ref[...])
        jj = jnp.minimum(j, 2)
        prev = carry_ref[jj]
        big_ref[0:V7X_SUBLANES] = jnp.where(s == 0, jnp.zeros_like(prev), prev)
        big_ref[V7X_SUBLANES:] = proj
        carry_ref[jj] = proj[ts - V7X_SUBLANES:]
        cw = cw_ref[...]
        base = V7X_SUBLANES - (CONV_K - 1)
        y = big_ref[base:base + ts] * cw[0:1]
        for t in range(1, CONV_K):
            y = y + big_ref[base + t:base + t + ts] * cw[t:t + 1]
        y = _silu(y)
        is_qk = j < 2
        for hh in range(GDN_HEADS):
            cs = slice(hh * GDN_DK, (hh + 1) * GDN_DK)
            yh = y[:, cs]
            yn = yh * lax.rsqrt(jnp.sum(yh * yh, axis=-1, keepdims=True) + EPS)
            o_ref[0, :, cs] = jnp.where(is_qk, yn, yh)


def _gdn_inproj(x, mod, g_row, w, wab, cw, ad, ts=TS_PROJ):
    bsz, seq, d = x.shape
    n_col = w.shape[1] // d
    return pl.pallas_call(
        functools.partial(_gdn_inproj_kernel, ts=ts),
        grid=(bsz, seq // ts, n_col),
        in_specs=[
            pl.BlockSpec((1, ts, d), lambda b, s, j: (b, s, 0)),
            pl.BlockSpec((1, 6, d), lambda b, s, j: (b, 0, 0)),
            pl.BlockSpec((1, d), lambda b, s, j: (0, 0)),
            pl.BlockSpec((d, d), lambda b, s, j: (0, j)),
            pl.BlockSpec((d, V7X_LANES), lambda b, s, j: (0, 0)),
            pl.BlockSpec((CONV_K, d), lambda b, s, j: (0, jnp.minimum(j, 2))),
            pl.BlockSpec((2, V7X_LANES), lambda b, s, j: (0, 0)),
        ],
        out_specs=[
            pl.BlockSpec((1, ts, d), lambda b, s, j: (b, s, j)),
            pl.BlockSpec((1, ts, V7X_LANES), lambda b, s, j: (b, s, 0)),
        ],
        out_shape=[
            jax.ShapeDtypeStruct((bsz, seq, n_col * d), F32),
            jax.ShapeDtypeStruct((bsz, seq, V7X_LANES), F32),
        ],
        scratch_shapes=[
            pltpu.VMEM((ts, d), BF16),
            pltpu.VMEM((ts + V7X_SUBLANES, d), F32),
            pltpu.VMEM((3, V7X_SUBLANES, d), F32),
        ],
        compiler_params=_params(("arbitrary", "arbitrary", "arbitrary")),
        name="gdn_inproj",
    )(x, mod, g_row, w, wab, cw, ad)


def _gdn_scan_kernel(q_ref, k_ref, v_ref, z_ref, gb_ref, og_ref, y_ref,
                     state_ref, wq_ref, u_ref, qk_ref, kd_ref, gl_ref, *, ts):
    c_sz = GDN_CHUNK

    @pl.when(pl.program_id(1) == 0)
    def _():
        state_ref[...] = jnp.zeros_like(state_ref)

    row = lax.broadcasted_iota(jnp.int32, (c_sz, 2 * c_sz), 0)
    col = lax.broadcasted_iota(jnp.int32, (c_sz, 2 * c_sz), 1) & (c_sz - 1)
    incl = row >= col
    strict = row > col
    ltri = incl[:, :c_sz].astype(F32)
    eye = (row == col).astype(F32)
    og = og_ref[...]
    q_scale = GDN_DK ** -0.5

    heads = range(GDN_HEADS)
    cols = [slice(h * GDN_DK, (h + 1) * GDN_DK) for h in heads]
    n_sub = GDN_CHUNKS_PER_STEP

    def prep(c, parked):
        rows, gbc, gcum, gcum_t = [], [], [], []
        for ci in range(n_sub):
            r0 = pl.multiple_of((c * n_sub + ci) * c_sz, c_sz)
            rows.append(pl.ds(r0, c_sz))
            gbc.append(gb_ref[0, rows[ci], :])
            gcum.append(_dot(ltri, gbc[ci], HIGHEST))
            gcum_t.append(jnp.concatenate([gcum[ci], gcum[ci]], axis=0).T)
        streams = [(ci, h) for ci in range(n_sub) for h in heads]
        qh = {s: q_ref[0, rows[s[0]], cols[s[1]]] * q_scale for s in streams}
        kh = {s: k_ref[0, rows[s[0]], cols[s[1]]] for s in streams}
        gc = {(ci, h): gcum[ci][:, h:h + 1] for ci, h in streams}
        beta = {(ci, h): gbc[ci][:, GDN_HEADS + h:GDN_HEADS + h + 1] for ci, h in streams}
        glast = {(ci, h): gcum[ci][c_sz - 1:c_sz, h:h + 1] for ci, h in streams}
        kb = {s: kh[s] * beta[s] for s in streams}
        def k_twice(s):
            kb16 = kh[s].astype(BF16)
            return jnp.concatenate([kb16, kb16], axis=0)

        kk = {s: _dot_nt(jnp.concatenate([kb[s], qh[s]], axis=0).astype(BF16), k_twice(s))
              for s in streams}
        yield
        eg = {s: jnp.exp(gc[s]) for s in streams}
        decay = {(ci, h): jnp.where(incl, jnp.exp(jnp.where(incl, gc[(ci, h)] - gcum_t[ci][h:h + 1, :],
                                                           0.0)), 0.0) for ci, h in streams}
        a_mat = {s: jnp.where(strict, kk[s][:c_sz] * decay[s], 0.0) for s in streams}
        qk = {s: (kk[s][c_sz:, :c_sz] * decay[s][:, :c_sz]).astype(BF16) for s in streams}
        x_mat = {s: eye - a_mat[s] for s in streams}
        p_mat = {s: _dot3_dup(a_mat[s], a_mat[s]) for s in streams}
        yield
        for _ in range(4):
            xp = {s: _dot3_dup(jnp.concatenate([x_mat[s], p_mat[s]], axis=0), p_mat[s]) for s in streams}
            x_mat = {s: x_mat[s] + xp[s][:c_sz] for s in streams}
            p_mat = {s: xp[s][c_sz:] for s in streams}
            yield
        xl = {s: _dot3_dup(x_mat[s], p_mat[s]) for s in streams}
        x_mat = {s: x_mat[s] + xl[s] for s in streams}
        yield
        sol = {s: _dot3_dup(x_mat[s], jnp.concatenate([v_ref[0, rows[s[0]], cols[s[1]]] * beta[s],
                                                       kb[s] * eg[s]], axis=1))
               for s in streams}
        for s in streams:
            parked[s] = (jnp.concatenate([sol[s][:, GDN_DK:], qh[s] * eg[s]], axis=0).astype(BF16),
                         sol[s][:, :GDN_DK], qk[s],
                         (kh[s] * jnp.exp(glast[s] - gc[s])).astype(BF16))
        for ci in range(n_sub):
            parked[ci] = jnp.exp(gcum[ci][c_sz - V7X_SUBLANES:, :])

    def park(parked):
        for ci in range(n_sub):
            for h in heads:
                wq_ref[ci, h], u_ref[ci, h], qk_ref[ci, h], kd_ref[ci, h] = parked[(ci, h)]
            gl_ref[ci] = parked[ci]

    def recur(c):
        for ci in range(n_sub):
            rows = pl.ds(pl.multiple_of((c * n_sub + ci) * c_sz, c_sz), c_sz)
            glast_e = gl_ref[ci][V7X_SUBLANES - 1:V7X_SUBLANES, :]
            st = [state_ref[h] for h in heads]
            wq = [_dot(wq_ref[ci, h], st[h].astype(BF16)) for h in heads]
            yield
            v_new = [(u_ref[ci, h] - wq[h][:c_sz]).astype(BF16) for h in heads]
            o = [wq[h][c_sz:] + _dot(qk_ref[ci, h], v_new[h]) for h in heads]
            for h in heads:
                state_ref[h] = st[h] * glast_e[:, h:h + 1] + _dot_tn(kd_ref[ci, h], v_new[h])
            yield
            for h in heads:
                on = o[h] * lax.rsqrt(jnp.mean(o[h] * o[h], axis=-1, keepdims=True) + EPS) * og
                y_ref[0, rows, cols[h]] = (on * _silu(z_ref[0, rows, cols[h]])).astype(BF16)
            yield

    def run_interleaved(*gens):
        gens = list(gens)
        while gens:
            for g in list(gens):
                if next(g, StopIteration) is StopIteration:
                    gens.remove(g)

    n_steps = ts // (c_sz * n_sub)
    parked0 = {}
    run_interleaved(prep(0, parked0))
    park(parked0)

    def body(c, carry):
        parked = {}
        run_interleaved(recur(c - 1), prep(c, parked))
        park(parked)
        return carry

    lax.fori_loop(1, n_steps, body, 0)
    run_interleaved(recur(n_steps - 1))


def _gdn_scan(qkvz, gb, og_row, ts=TS_SCAN):
    bsz, seq, _ = qkvz.shape
    d = GDN_HEADS * GDN_DK
    col = lambda j: pl.BlockSpec((1, ts, d), lambda b, s: (b, s, j))
    return pl.pallas_call(
        functools.partial(_gdn_scan_kernel, ts=ts),
        grid=(bsz, seq // ts),
        in_specs=[col(0), col(1), col(2), col(3),
                  pl.BlockSpec((1, ts, V7X_LANES), lambda b, s: (b, s, 0)),
                  pl.BlockSpec((1, GDN_DK), lambda b, s: (0, 0))],
        out_specs=pl.BlockSpec((1, ts, d), lambda b, s: (b, s, 0)),
        out_shape=jax.ShapeDtypeStruct((bsz, seq, d), BF16),
        scratch_shapes=[
            pltpu.VMEM((GDN_HEADS, GDN_DK, GDN_DK), F32),
            pltpu.VMEM((GDN_CHUNKS_PER_STEP, GDN_HEADS, 2 * GDN_CHUNK, GDN_DK), BF16),
            pltpu.VMEM((GDN_CHUNKS_PER_STEP, GDN_HEADS, GDN_CHUNK, GDN_DK), F32),
            pltpu.VMEM((GDN_CHUNKS_PER_STEP, GDN_HEADS, GDN_CHUNK, GDN_CHUNK), BF16),
            pltpu.VMEM((GDN_CHUNKS_PER_STEP, GDN_HEADS, GDN_CHUNK, GDN_DK), BF16),
            pltpu.VMEM((GDN_CHUNKS_PER_STEP, V7X_SUBLANES, V7X_LANES), F32),
        ],
        compiler_params=_params(("arbitrary", "arbitrary")),
        name="gdn_scan",
    )(qkvz, qkvz, qkvz, qkvz, gb, og_row)


def _mixer_out_mlp_kernel(x_ref, y_ref, mod_ref, wo_ref, g_ref, w1_ref, w2_ref, fg_ref, o_ref,
                          *, final_norm):
    m = mod_ref[0]
    x1 = x_ref[0] + m[2:3] * _dot(y_ref[0], wo_ref[...])
    h = _modulated_norm(x1, g_ref[...], m[3:4], m[4:5]).astype(BF16)
    d_ff = w1_ref.shape[1]
    acc = None
    for c in range(d_ff // FF_CHUNK):
        cs = slice(c * FF_CHUNK, (c + 1) * FF_CHUNK)
        hid = jnp.maximum(_dot(h, w1_ref[:, cs]), 0.0)
        part = _dot((hid * hid).astype(BF16), w2_ref[cs, :])
        acc = part if acc is None else acc + part
    x2 = x1 + m[5:6] * acc
    if final_norm:
        x2 = x2 * lax.rsqrt(jnp.mean(x2 * x2, axis=-1, keepdims=True) + EPS) * fg_ref[...]
    o_ref[0] = x2


def _mixer_out_mlp(x, y, mod, wo, g_row, w1, w2, fg_row, final_norm, tm=TM_MLP):
    bsz, seq, d = x.shape
    d_ff = w1.shape[1]
    const = lambda shape: pl.BlockSpec(shape, lambda b, s: (0,) * len(shape),
                                       pipeline_mode=pl.Buffered(1))
    return pl.pallas_call(
        functools.partial(_mixer_out_mlp_kernel, final_norm=final_norm),
        grid=(bsz, seq // tm),
        in_specs=[
            pl.BlockSpec((1, tm, d), lambda b, s: (b, s, 0)),
            pl.BlockSpec((1, tm, d), lambda b, s: (b, s, 0)),
            pl.BlockSpec((1, 6, d), lambda b, s: (b, 0, 0)),
            const((d, d)),
            const((1, d)),
            const((d, d_ff)),
            const((d_ff, d)),
            const((1, d)),
        ],
        out_specs=pl.BlockSpec((1, tm, d), lambda b, s: (b, s, 0)),
        out_shape=jax.ShapeDtypeStruct((bsz, seq, d), F32),
        compiler_params=_params(("arbitrary", "arbitrary")),
        name="mixer_out_mlp",
    )(x, y, mod, wo, g_row, w1, w2, fg_row)


def _rope(y, cos_t, sin_t):
    outs = []
    for gidx in range(y.shape[1] // DIFF_DH):
        yh = y[:, gidx * DIFF_DH:(gidx + 1) * DIFF_DH]
        outs.append(yh * cos_t + pltpu.roll(yh, DIFF_DH // 2, 1) * sin_t)
    return outs


def _attn_inproj_kernel(x_ref, mod_ref, kvmod_ref, gq_ref, gkv_ref, w_ref, cos_ref, sin_ref,
                        o_ref, hq_ref, hk_ref):
    j = pl.program_id(2)

    @pl.when(j == 0)
    def _():
        x = x_ref[0]
        xn = x * lax.rsqrt(jnp.mean(x * x, axis=-1, keepdims=True) + EPS)
        m = mod_ref[0]
        km = kvmod_ref[0]
        hq_ref[...] = (xn * (gq_ref[...] * (1.0 + m[1:2])) + m[0:1]).astype(BF16)
        hk_ref[...] = (xn * (gkv_ref[...] * (1.0 + km[1:2])) + km[0:1]).astype(BF16)

    @pl.when(j == 0)
    def _():
        y = _dot(hq_ref[...], w_ref[...])
        q_scale = (DIFF_DH ** -0.5) * LOG2E
        for gidx, yr in enumerate(_rope(y, cos_ref[...], sin_ref[...])):
            o_ref[0, :, gidx * DIFF_DH:(gidx + 1) * DIFF_DH] = (yr * q_scale).astype(BF16)

    @pl.when(j == 1)
    def _():
        y = _dot(hk_ref[...], w_ref[...])
        for gidx, yr in enumerate(_rope(y, cos_ref[...], sin_ref[...])):
            o_ref[0, :, gidx * DIFF_DH:(gidx + 1) * DIFF_DH] = yr.astype(BF16)

    @pl.when(j == 2)
    def _():
        o_ref[0] = _dot(hk_ref[...], w_ref[...]).astype(BF16)


def _attn_inproj(x, mod, kvmod, gq_row, gkv_row, w, cos_t, sin_t, ts=TS_PROJ):
    bsz, seq, d = x.shape
    n_col = w.shape[1] // d
    return pl.pallas_call(
        _attn_inproj_kernel,
        grid=(bsz, seq // ts, n_col),
        in_specs=[
            pl.BlockSpec((1, ts, d), lambda b, s, j: (b, s, 0)),
            pl.BlockSpec((1, 6, d), lambda b, s, j: (b, 0, 0)),
            pl.BlockSpec((1, 2, d), lambda b, s, j: (b, 0, 0)),
            pl.BlockSpec((1, d), lambda b, s, j: (0, 0)),
            pl.BlockSpec((1, d), lambda b, s, j: (0, 0)),
            pl.BlockSpec((d, d), lambda b, s, j: (0, j)),
            pl.BlockSpec((ts, DIFF_DH), lambda b, s, j: (s, 0)),
            pl.BlockSpec((ts, DIFF_DH), lambda b, s, j: (s, 0)),
        ],
        out_specs=pl.BlockSpec((1, ts, d), lambda b, s, j: (b, s, j)),
        out_shape=jax.ShapeDtypeStruct((bsz, seq, n_col * d), BF16),
        scratch_shapes=[pltpu.VMEM((ts, d), BF16), pltpu.VMEM((ts, d), BF16)],
        compiler_params=_params(("arbitrary", "arbitrary", "arbitrary")),
        name="attn_inproj",
    )(x, mod, kvmod, gq_row, gkv_row, w, cos_t, sin_t)


def _diff_attn_kernel(q_ref, k_ref, v_ref, lam_ref, g_ref, o_ref, m_ref, l_ref, acc_ref,
                      s_ref, mx_ref, *, tq, tk, lam_init):
    qi = pl.program_id(2)
    m_ref[...] = jnp.full(m_ref.shape, -jnp.inf, F32)
    l_ref[...] = jnp.zeros_like(l_ref)
    acc_ref[...] = jnp.zeros_like(acc_ref)

    rt = ATTN_ROW_TILE
    n_rt = tq // rt

    def chunks(width):
        return [slice(j * V7X_LANES, (j + 1) * V7X_LANES) for j in range(width // V7X_LANES)]

    def scores_rows(kblk, slot, rq, width=None):
        rk = pl.ds(pl.multiple_of(kblk * tk, tk), tk if width is None else width)
        for mi in range(2):
            cs = slice(mi * DIFF_DH, (mi + 1) * DIFF_DH)
            s = _dot_nt(q_ref[0, rq, cs], k_ref[0, rk, cs])
            if width is None:
                s_ref[slot, mi, rq, :] = s
                mx_ref[slot, mi, rq, :] = functools.reduce(jnp.maximum, [s[:, c] for c in chunks(tk)])
            else:
                s_ref[slot, mi, rq, 0:width] = s

    def softmax_rows(slot, rq, width, diag_row0):
        if diag_row0 is not None:
            r = lax.broadcasted_iota(jnp.int32, (rt, V7X_LANES), 0)
            c = lax.broadcasted_iota(jnp.int32, (rt, V7X_LANES), 1)
            diff = r - c + diag_row0
        out = []
        for mi in range(2):
            sc = [s_ref[slot, mi, rq, c_] for c_ in chunks(width)]
            if diag_row0 is not None:
                sc = [jnp.where(diff >= c_.start, x, -jnp.inf) for c_, x in zip(chunks(width), sc)]
                mx = functools.reduce(jnp.maximum, sc)
            else:
                mx = mx_ref[slot, mi, rq, :]
            m_old = m_ref[mi, rq, :]
            m_new = jnp.maximum(m_old, jnp.max(mx, axis=-1, keepdims=True))
            alpha = jnp.exp2(m_old - m_new)
            m_ref[mi, rq, :] = m_new
            ps = [jnp.exp2(x - m_new) for x in sc]
            l_ref[mi, rq, :] = alpha * l_ref[mi, rq, :] + functools.reduce(jnp.add, ps)
            out.append((alpha, jnp.concatenate(ps, axis=1).astype(BF16)))
        return out

    def pv_rows(mi, rq, alpha, p, v):
        acc_ref[mi, rq, :] = jnp.concatenate([alpha, alpha], axis=1) * acc_ref[mi, rq, :] + _dot(p, v)

    def row_tile(t):
        return pl.ds(pl.multiple_of(t * rt, rt), rt)

    blocks_per_k = tk // tq
    n_full = qi // blocks_per_k
    rem = qi - n_full * blocks_per_k

    kdiag = pl.multiple_of(n_full * tk, tk)
    for rem_static in range(blocks_per_k):
        @pl.when(rem == rem_static)
        def _():
            for t in range(n_rt):
                scores_rows(n_full, 0, pl.ds(t * rt, rt), width=rem_static * tq + (t + 1) * rt)
            for t in range(n_rt):
                rq = pl.ds(t * rt, rt)
                first_visible = rem_static * tq + t * rt
                width = first_visible + rt
                v = v_ref[0, pl.ds(kdiag, width), :]
                for mi, (alpha, p) in enumerate(softmax_rows(0, rq, width, first_visible)):
                    pv_rows(mi, rq, alpha, p, v)
                scores_rows(0, 1, rq)

    def visible_block(i, produce_next):
        slot = (i + 1) & 1
        v = v_ref[0, pl.ds(pl.multiple_of(i * tk, tk), tk), :]

        def tile(t, c):
            rq = row_tile(t)
            for mi, (alpha, p) in enumerate(softmax_rows(slot, rq, tk, None)):
                pv_rows(mi, rq, alpha, p, v)
            if produce_next:
                scores_rows(i + 1, 1 - slot, rq)
            return c

        lax.fori_loop(0, n_rt, tile, 0)

    def visible_step(i, carry):
        visible_block(i, True)
        return carry

    lax.fori_loop(0, n_full - 1, visible_step, 0)

    @pl.when(n_full > 0)
    def _():
        visible_block(n_full - 1, False)

    lp = lam_ref[...]
    lam = (jnp.exp(jnp.sum(lp[0:1] * lp[1:2], axis=-1, keepdims=True))
           - jnp.exp(jnp.sum(lp[2:3] * lp[3:4], axis=-1, keepdims=True)) + lam_init)
    l0 = jnp.sum(l_ref[0], axis=-1, keepdims=True)
    l1 = jnp.sum(l_ref[1], axis=-1, keepdims=True)
    o = acc_ref[0] / l0 - lam * (acc_ref[1] / l1)
    on = o * lax.rsqrt(jnp.mean(o * o, axis=-1, keepdims=True) + EPS) * g_ref[...]
    o_ref[0] = (on * (1.0 - lam_init)).astype(BF16)


def _diff_attn(qkv, lam_params, g_row, lam_init, tq=TQ_ATTN, tk=TK_ATTN):
    bsz, seq, _ = qkv.shape
    hw = 2 * DIFF_DH
    assert tk % tq == 0 and seq % tk == 0
    return pl.pallas_call(
        functools.partial(_diff_attn_kernel, tq=tq, tk=tk, lam_init=lam_init),
        grid=(bsz, DIFF_HEADS, seq // tq),
        in_specs=[
            pl.BlockSpec((1, tq, hw), lambda b, h, i: (b, i, h)),
            pl.BlockSpec((1, seq, hw), lambda b, h, i: (b, 0, DIFF_HEADS + h),
                         pipeline_mode=pl.Buffered(1)),
            pl.BlockSpec((1, seq, hw), lambda b, h, i: (b, 0, 2 * DIFF_HEADS + h),
                         pipeline_mode=pl.Buffered(1)),
            pl.BlockSpec((4, DIFF_DH), lambda b, h, i: (0, 0)),
            pl.BlockSpec((1, hw), lambda b, h, i: (0, 0)),
        ],
        out_specs=pl.BlockSpec((1, tq, hw), lambda b, h, i: (b, i, h)),
        out_shape=jax.ShapeDtypeStruct((bsz, seq, DIFF_HEADS * hw), BF16),
        scratch_shapes=[
            pltpu.VMEM((2, tq, V7X_LANES), F32),
            pltpu.VMEM((2, tq, V7X_LANES), F32),
            pltpu.VMEM((2, tq, hw), F32),
            pltpu.VMEM((2, 2, tq, tk), F32),
            pltpu.VMEM((2, 2, tq, V7X_LANES), F32),
        ],
        compiler_params=_params(("arbitrary", "arbitrary", "arbitrary")),
        name="diff_attn",
    )(qkv, qkv, qkv, lam_params, g_row)


def _rope_tables(seq):
    pos = jnp.arange(seq, dtype=F32)
    inv_freq = ROPE_THETA ** (-jnp.arange(0, ROT_DIM, 2, dtype=F32) / ROT_DIM)
    freqs = pos[:, None] * inv_freq[None, :]
    cos, sin = jnp.cos(freqs), jnp.sin(freqs)
    pad = (DIFF_DH - ROT_DIM) // 2
    ones, zeros = jnp.ones((seq, pad), F32), jnp.zeros((seq, pad), F32)
    cos_t = jnp.concatenate([cos, ones, cos, ones], axis=1)
    sin_t = jnp.concatenate([-sin, zeros, sin, zeros], axis=1)
    return cos_t, sin_t


def kernel(x, c, mod_w, mod_b, norm_mix_g, norm_mlp_g, a_w_in, a_conv_w, a_log, a_dt_bias, a_out_norm_g, a_w_out,
           kv_mod_w, kv_mod_b, kv_norm_g, kv_w, b_w_q, b_lambda, b_subln_g, b_w_out, mlp_w1, mlp_w2, final_g):
    bsz, seq, d = x.shape
    depth = mod_w.shape[0]
    assert depth == 2 and a_w_in.shape[0] == 1 and b_w_q.shape[0] == 1
    assert seq % TS_PROJ == 0 and seq % TS_SCAN == 0 and seq % TM_MLP == 0 and seq % TQ_ATTN == 0

    c_t = c.T
    mod = _mod_call(c_t, mod_w, mod_b[:, None, :]).reshape(depth, bsz, 6, d)
    kvmod = _mod_call(c_t, kv_mod_w[None], kv_mod_b[None, None, :]).reshape(bsz, 2, d)

    row = lambda v: v.reshape(1, -1).astype(F32)

    w_in = a_w_in[0]
    n_main = 4 * d
    w_main = w_in[:, :n_main].astype(BF16)
    w_ab = jnp.pad(w_in[:, n_main:], ((0, 0), (0, V7X_LANES - 2 * GDN_HEADS))).astype(BF16)
    ad = jnp.pad(jnp.stack([a_log[0], a_dt_bias[0]]).astype(F32), ((0, 0), (0, V7X_LANES - GDN_HEADS)))
    qkvz, gb = _gdn_inproj(x, mod[0], row(norm_mix_g[0]), w_main, w_ab, a_conv_w[0].astype(F32), ad)
    y0 = _gdn_scan(qkvz, gb, row(a_out_norm_g[0]))
    x = _mixer_out_mlp(x, y0, mod[0], a_w_out[0].astype(BF16), row(norm_mlp_g[0]),
                       mlp_w1[0].astype(BF16), mlp_w2[0].astype(BF16), row(final_g), False)

    lam_init = 0.8 - 0.6 * math.exp(-0.3 * 1)
    def rope_order(w):
        half, mid = ROT_DIM // 2, DIFF_DH // 2
        g = w.reshape(d, -1, DIFF_DH)
        g = jnp.concatenate([g[..., :half], g[..., ROT_DIM:mid + half], g[..., half:ROT_DIM],
                             g[..., mid + half:]], axis=-1)
        return g.reshape(w.shape)

    n_k = 2 * DIFF_HEADS * DIFF_DH
    w_qkv = jnp.concatenate([rope_order(b_w_q[0]), rope_order(kv_w[:, :n_k]), kv_w[:, n_k:]],
                            axis=1).astype(BF16)
    cos_t, sin_t = _rope_tables(seq)
    qkv = _attn_inproj(x, mod[1], kvmod, row(norm_mix_g[1]), row(kv_norm_g), w_qkv, cos_t, sin_t)
    y1 = _diff_attn(qkv, b_lambda[0].astype(F32), row(b_subln_g[0]), lam_init)
    x = _mixer_out_mlp(x, y1, mod[1], b_w_out[0].astype(BF16), row(norm_mlp_g[1]),
                       mlp_w1[1].astype(BF16), mlp_w2[1].astype(BF16), row(final_g), True)
    return x
```

```python
import functools
import math

import jax
import jax.numpy as jnp
from jax import lax
from jax.experimental import pallas as pl
from jax.experimental.pallas import tpu as pltpu

F32 = jnp.float32
BF16 = jnp.bfloat16
HIGHEST = lax.Precision.HIGHEST

V7X_LANES = 128
V7X_SUBLANES = 8
V7X_VMEM_LIMIT_BYTES = 56 * 1024 * 1024

EPS = 1e-6
GDN_HEADS = 8
GDN_DK = 128
GDN_CHUNK = 64
GDN_CHUNKS_PER_STEP = 2
CONV_K = 4
DIFF_HEADS = 4
DIFF_DH = 128
ROT_DIM = DIFF_DH // 4
ROPE_THETA = 500000.0
LOG2E = 1.4426950408889634

TS_PROJ = 512
TS_SCAN = 1024
TM_MLP = 512
TQ_ATTN = 512
TK_ATTN = 2048
ATTN_ROW_TILE = 256
FF_CHUNK = 1024


def _params(sem, vmem=V7X_VMEM_LIMIT_BYTES, flags=None):
    return pltpu.CompilerParams(dimension_semantics=sem, vmem_limit_bytes=vmem, flags=flags)


def _sigmoid(x):
    return 1.0 / (1.0 + jnp.exp(-x))


def _silu(x):
    return x * _sigmoid(x)


def _softplus(x):
    return jnp.maximum(x, 0.0) + jnp.log1p(jnp.exp(-jnp.abs(x)))


def _dot(a, b, precision=None):
    return jnp.dot(a, b, preferred_element_type=F32, precision=precision)


def _dot_nt(a, b):
    return lax.dot_general(a, b, (((1,), (1,)), ((), ())), preferred_element_type=F32)


def _dot_tn(a, b):
    return lax.dot_general(a, b, (((0,), (0,)), ((), ())), preferred_element_type=F32)


def _dot3_dup(a_dup, b):
    k = b.shape[0]
    a_hi = a_dup.astype(BF16)
    a_hi32 = a_hi.astype(F32)
    low = lax.broadcasted_iota(jnp.int32, a_dup.shape, 1) < k
    hi_lo = jnp.where(low, a_hi32, a_dup - a_hi32).astype(BF16)
    b_hi = b.astype(BF16)
    b_lo = (b - b_hi.astype(F32)).astype(BF16)
    rhs = jnp.concatenate([b_hi, b_hi, b_lo, jnp.zeros_like(b_hi)], axis=0)
    return _dot(jnp.concatenate([hi_lo, a_hi], axis=1), rhs)


def _mod_kernel(ct_ref, w_ref, b_ref, o_ref, *, batch):
    cs = _silu(ct_ref[...])
    w = w_ref[0]
    for b in range(batch):
        o_ref[0, b:b + 1, :] = jnp.sum(w * cs[:, b:b + 1], axis=0, keepdims=True) + b_ref[0]


def _mod_call(c_t, w, bias, tn=2048):
    n_l, d, n = w.shape
    batch = c_t.shape[1]
    return pl.pallas_call(
        functools.partial(_mod_kernel, batch=batch),
        grid=(n_l, n // tn),
        in_specs=[
            pl.BlockSpec((d, batch), lambda l, j: (0, 0)),
            pl.BlockSpec((1, d, tn), lambda l, j: (l, 0, j)),
            pl.BlockSpec((1, 1, tn), lambda l, j: (l, 0, j)),
        ],
        out_specs=pl.BlockSpec((1, batch, tn), lambda l, j: (l, 0, j)),
        out_shape=jax.ShapeDtypeStruct((n_l, batch, n), F32),
        compiler_params=_params(("arbitrary", "arbitrary")),
        name="mod_vectors",
    )(c_t, w, bias)


def _modulated_norm(x, g_row, shift_row, scale_row):
    rstd = lax.rsqrt(jnp.mean(x * x, axis=-1, keepdims=True) + EPS)
    return (x * rstd) * (g_row * (1.0 + scale_row)) + shift_row


def _gdn_inproj_kernel(x_ref, mod_ref, g_ref, w_ref, wab_ref, cw_ref, ad_ref,
                       o_ref, gb_ref, big_ref, *, ts):
    d = x_ref.shape[2]

    @pl.when(pl.program_id(1) == 0)
    def _():
        big_ref[:, ts:ts + V7X_SUBLANES] = jnp.zeros((3, V7X_SUBLANES, d), F32)

    m = mod_ref[0]
    hb = _modulated_norm(x_ref[0], g_ref[...], m[0:1], m[1:2]).astype(BF16)
    ab = _dot(hb, wab_ref[...])
    ad = ad_ref[...]
    g = -jnp.exp(ad[0:1]) * _softplus(ab + ad[1:2])
    lane = lax.broadcasted_iota(jnp.int32, ab.shape, 1)
    gb_ref[0] = jnp.where(lane < GDN_HEADS, g, _sigmoid(ab))

    base = V7X_SUBLANES - (CONV_K - 1)
    for j in range(3):
        cs_j = slice(j * d, (j + 1) * d)
        proj = _dot(hb, w_ref[:, cs_j])
        big_ref[j, 0:V7X_SUBLANES] = big_ref[j, ts:ts + V7X_SUBLANES]
        big_ref[j, V7X_SUBLANES:] = proj
        cw = cw_ref[:, cs_j]
        y = big_ref[j, base:base + ts] * cw[0:1]
        for t in range(1, CONV_K):
            y = y + big_ref[j, base + t:base + t + ts] * cw[t:t + 1]
        y = _silu(y)
        if j == 2:
            o_ref[0, :, cs_j] = y
        else:
            for hh in range(GDN_HEADS):
                cs = slice(hh * GDN_DK, (hh + 1) * GDN_DK)
                yh = y[:, cs]
                o_ref[0, :, j * d + hh * GDN_DK:j * d + (hh + 1) * GDN_DK] = (
                    yh * lax.rsqrt(jnp.sum(yh * yh, axis=-1, keepdims=True) + EPS))
    o_ref[0, :, 3 * d:4 * d] = _dot(hb, w_ref[:, 3 * d:4 * d])


def _gdn_inproj(x, mod, g_row, w, wab, cw, ad, ts=TS_PROJ):
    bsz, seq, d = x.shape
    n_out = w.shape[1]
    const = lambda shape: pl.BlockSpec(shape, lambda b, s: (0,) * len(shape),
                                       pipeline_mode=pl.Buffered(1))
    return pl.pallas_call(
        functools.partial(_gdn_inproj_kernel, ts=ts),
        grid=(bsz, seq // ts),
        in_specs=[
            pl.BlockSpec((1, ts, d), lambda b, s: (b, s, 0)),
            pl.BlockSpec((1, 6, d), lambda b, s: (b, 0, 0)),
            const((1, d)),
            const((d, n_out)),
            const((d, V7X_LANES)),
            const((CONV_K, 3 * d)),
            const((2, V7X_LANES)),
        ],
        out_specs=[
            pl.BlockSpec((1, ts, n_out), lambda b, s: (b, s, 0)),
            pl.BlockSpec((1, ts, V7X_LANES), lambda b, s: (b, s, 0)),
        ],
        out_shape=[
            jax.ShapeDtypeStruct((bsz, seq, n_out), F32),
            jax.ShapeDtypeStruct((bsz, seq, V7X_LANES), F32),
        ],
        scratch_shapes=[pltpu.VMEM((3, ts + V7X_SUBLANES, d), F32)],
        compiler_params=_params(("arbitrary", "arbitrary")),
        name="gdn_inproj",
    )(x, mod, g_row, w, wab, cw, ad)


def _gdn_scan_kernel(q_ref, k_ref, v_ref, z_ref, gb_ref, og_ref, y_ref,
                     state_ref, wq_ref, u_ref, qk_ref, kd_ref, gl_ref, *, ts):
    c_sz = GDN_CHUNK

    @pl.when(pl.program_id(1) == 0)
    def _():
        state_ref[...] = jnp.zeros_like(state_ref)

    row = lax.broadcasted_iota(jnp.int32, (c_sz, 2 * c_sz), 0)
    col = lax.broadcasted_iota(jnp.int32, (c_sz, 2 * c_sz), 1) & (c_sz - 1)
    incl = row >= col
    strict = row > col
    ltri = incl[:, :c_sz].astype(F32)
    eye = (row == col).astype(F32)
    og = og_ref[...]
    q_scale = GDN_DK ** -0.5

    heads = range(GDN_HEADS)
    cols = [slice(h * GDN_DK, (h + 1) * GDN_DK) for h in heads]
    n_sub = GDN_CHUNKS_PER_STEP

    def prep(c, parked):
        rows, gbc, gcum, gcum_t = [], [], [], []
        for ci in range(n_sub):
            r0 = pl.multiple_of((c * n_sub + ci) * c_sz, c_sz)
            rows.append(pl.ds(r0, c_sz))
            gbc.append(gb_ref[0, rows[ci], :])
            gcum.append(_dot(ltri, gbc[ci], HIGHEST))
            gcum_t.append(jnp.concatenate([gcum[ci], gcum[ci]], axis=0).T)
        streams = [(ci, h) for ci in range(n_sub) for h in heads]
        qh = {s: q_ref[0, rows[s[0]], cols[s[1]]] * q_scale for s in streams}
        kh = {s: k_ref[0, rows[s[0]], cols[s[1]]] for s in streams}
        gc = {(ci, h): gcum[ci][:, h:h + 1] for ci, h in streams}
        beta = {(ci, h): gbc[ci][:, GDN_HEADS + h:GDN_HEADS + h + 1] for ci, h in streams}
        glast = {(ci, h): gcum[ci][c_sz - 1:c_sz, h:h + 1] for ci, h in streams}
        kb = {s: kh[s] * beta[s] for s in streams}
        def k_twice(s):
            kb16 = kh[s].astype(BF16)
            return jnp.concatenate([kb16, kb16], axis=0)

        kk = {s: _dot_nt(jnp.concatenate([kb[s], qh[s]], axis=0).astype(BF16), k_twice(s))
              for s in streams}
        yield
        eg = {s: jnp.exp(gc[s]) for s in streams}
        decay = {(ci, h): jnp.where(incl, jnp.exp(jnp.where(incl, gc[(ci, h)] - gcum_t[ci][h:h + 1, :],
                                                           0.0)), 0.0) for ci, h in streams}
        a_mat = {s: jnp.where(strict, kk[s][:c_sz] * decay[s], 0.0) for s in streams}
        qk = {s: (kk[s][c_sz:, :c_sz] * decay[s][:, :c_sz]).astype(BF16) for s in streams}
        x_mat = {s: eye - a_mat[s] for s in streams}
        p_mat = {s: _dot3_dup(a_mat[s], a_mat[s]) for s in streams}
        yield
        for _ in range(4):
            xp = {s: _dot3_dup(jnp.concatenate([x_mat[s], p_mat[s]], axis=0), p_mat[s]) for s in streams}
            x_mat = {s: x_mat[s] + xp[s][:c_sz] for s in streams}
            p_mat = {s: xp[s][c_sz:] for s in streams}
            yield
        xl = {s: _dot3_dup(x_mat[s], p_mat[s]) for s in streams}
        x_mat = {s: x_mat[s] + xl[s] for s in streams}
        yield
        sol = {s: _dot3_dup(x_mat[s], jnp.concatenate([v_ref[0, rows[s[0]], cols[s[1]]] * beta[s],
                                                       kb[s] * eg[s]], axis=1))
               for s in streams}
        for s in streams:
            parked[s] = (jnp.concatenate([sol[s][:, GDN_DK:], qh[s] * eg[s]], axis=0).astype(BF16),
                         sol[s][:, :GDN_DK], qk[s],
                         (kh[s] * jnp.exp(glast[s] - gc[s])).astype(BF16))
        for ci in range(n_sub):
            parked[ci] = jnp.exp(gcum[ci][c_sz - V7X_SUBLANES:, :])

    def park(parked):
        for ci in range(n_sub):
            for h in heads:
                wq_ref[ci, h], u_ref[ci, h], qk_ref[ci, h], kd_ref[ci, h] = parked[(ci, h)]
            gl_ref[ci] = parked[ci]

    def recur(c):
        for ci in range(n_sub):
            rows = pl.ds(pl.multiple_of((c * n_sub + ci) * c_sz, c_sz), c_sz)
            glast_e = gl_ref[ci][V7X_SUBLANES - 1:V7X_SUBLANES, :]
            st = [state_ref[h] for h in heads]
            wq = [_dot(wq_ref[ci, h], st[h].astype(BF16)) for h in heads]
            yield
            v_new = [(u_ref[ci, h] - wq[h][:c_sz]).astype(BF16) for h in heads]
            o = [wq[h][c_sz:] + _dot(qk_ref[ci, h], v_new[h]) for h in heads]
            for h in heads:
                state_ref[h] = st[h] * glast_e[:, h:h + 1] + _dot_tn(kd_ref[ci, h], v_new[h])
            yield
            for h in heads:
                on = o[h] * lax.rsqrt(jnp.mean(o[h] * o[h], axis=-1, keepdims=True) + EPS) * og
                y_ref[0, rows, cols[h]] = (on * _silu(z_ref[0, rows, cols[h]])).astype(BF16)
            yield

    def run_interleaved(*gens):
        gens = list(gens)
        while gens:
            for g in list(gens):
                if next(g, StopIteration) is StopIteration:
                    gens.remove(g)

    n_steps = ts // (c_sz * n_sub)
    parked0 = {}
    run_interleaved(prep(0, parked0))
    park(parked0)

    def body(c, carry):
        parked = {}
        run_interleaved(recur(c - 1), prep(c, parked))
        park(parked)
        return carry

    lax.fori_loop(1, n_steps, body, 0)
    run_interleaved(recur(n_steps - 1))


def _gdn_scan(qkvz, gb, og_row, ts=TS_SCAN):
    bsz, seq, _ = qkvz.shape
    d = GDN_HEADS * GDN_DK
    col = lambda j: pl.BlockSpec((1, ts, d), lambda b, s: (b, s, j))
    return pl.pallas_call(
        functools.partial(_gdn_scan_kernel, ts=ts),
        grid=(bsz, seq // ts),
        in_specs=[col(0), col(1), col(2), col(3),
                  pl.BlockSpec((1, ts, V7X_LANES), lambda b, s: (b, s, 0)),
                  pl.BlockSpec((1, GDN_DK), lambda b, s: (0, 0))],
        out_specs=pl.BlockSpec((1, ts, d), lambda b, s: (b, s, 0)),
        out_shape=jax.ShapeDtypeStruct((bsz, seq, d), BF16),
        scratch_shapes=[
            pltpu.VMEM((GDN_HEADS, GDN_DK, GDN_DK), F32),
            pltpu.VMEM((GDN_CHUNKS_PER_STEP, GDN_HEADS, 2 * GDN_CHUNK, GDN_DK), BF16),
            pltpu.VMEM((GDN_CHUNKS_PER_STEP, GDN_HEADS, GDN_CHUNK, GDN_DK), F32),
            pltpu.VMEM((GDN_CHUNKS_PER_STEP, GDN_HEADS, GDN_CHUNK, GDN_CHUNK), BF16),
            pltpu.VMEM((GDN_CHUNKS_PER_STEP, GDN_HEADS, GDN_CHUNK, GDN_DK), BF16),
            pltpu.VMEM((GDN_CHUNKS_PER_STEP, V7X_SUBLANES, V7X_LANES), F32),
        ],
        compiler_params=_params(("arbitrary", "arbitrary")),
        name="gdn_scan",
    )(qkvz, qkvz, qkvz, qkvz, gb, og_row)


def _mixer_out_mlp_kernel(x_ref, y_ref, mod_ref, wo_ref, g_ref, w1_ref, w2_ref, fg_ref, o_ref,
                          *, final_norm):
    m = mod_ref[0]
    x1 = x_ref[0] + m[2:3] * _dot(y_ref[0], wo_ref[...])
    h = _modulated_norm(x1, g_ref[...], m[3:4], m[4:5]).astype(BF16)
    d_ff = w1_ref.shape[1]
    acc = None
    for c in range(d_ff // FF_CHUNK):
        cs = slice(c * FF_CHUNK, (c + 1) * FF_CHUNK)
        hid = jnp.maximum(_dot(h, w1_ref[:, cs]), 0.0)
        part = _dot((hid * hid).astype(BF16), w2_ref[cs, :])
        acc = part if acc is None else acc + part
    x2 = x1 + m[5:6] * acc
    if final_norm:
        x2 = x2 * lax.rsqrt(jnp.mean(x2 * x2, axis=-1, keepdims=True) + EPS) * fg_ref[...]
    o_ref[0] = x2


def _mixer_out_mlp(x, y, mod, wo, g_row, w1, w2, fg_row, final_norm, tm=TM_MLP):
    bsz, seq, d = x.shape
    d_ff = w1.shape[1]
    const = lambda shape: pl.BlockSpec(shape, lambda b, s: (0,) * len(shape),
                                       pipeline_mode=pl.Buffered(1))
    return pl.pallas_call(
        functools.partial(_mixer_out_mlp_kernel, final_norm=final_norm),
        grid=(bsz, seq // tm),
        in_specs=[
            pl.BlockSpec((1, tm, d), lambda b, s: (b, s, 0)),
            pl.BlockSpec((1, tm, d), lambda b, s: (b, s, 0)),
            pl.BlockSpec((1, 6, d), lambda b, s: (b, 0, 0)),
            const((d, d)),
            const((1, d)),
            const((d, d_ff)),
            const((d_ff, d)),
            const((1, d)),
        ],
        out_specs=pl.BlockSpec((1, tm, d), lambda b, s: (b, s, 0)),
        out_shape=jax.ShapeDtypeStruct((bsz, seq, d), F32),
        compiler_params=_params(("arbitrary", "arbitrary")),
        name="mixer_out_mlp",
    )(x, y, mod, wo, g_row, w1, w2, fg_row)


def _rope(y, cos_t, sin_t):
    outs = []
    for gidx in range(y.shape[1] // DIFF_DH):
        yh = y[:, gidx * DIFF_DH:(gidx + 1) * DIFF_DH]
        outs.append(yh * cos_t + pltpu.roll(yh, DIFF_DH // 2, 1) * sin_t)
    return outs


def _attn_inproj_kernel(x_ref, mod_ref, kvmod_ref, gq_ref, gkv_ref, w_ref, tile_cs_ref, row_cs_ref,
                        o_ref):
    d = x_ref.shape[2]
    x = x_ref[0]
    xn = x * lax.rsqrt(jnp.mean(x * x, axis=-1, keepdims=True) + EPS)
    m = mod_ref[0]
    km = kvmod_ref[0]
    hq = (xn * (gq_ref[...] * (1.0 + m[1:2])) + m[0:1]).astype(BF16)
    hk = (xn * (gkv_ref[...] * (1.0 + km[1:2])) + km[0:1]).astype(BF16)

    tc, tsn = tile_cs_ref[0, 0:1, :], tile_cs_ref[0, 1:2, :]
    rc, rsn = row_cs_ref[0], row_cs_ref[1]
    lane = lax.broadcasted_iota(jnp.int32, rc.shape, 1)
    cos_t = tc * rc - tsn * rsn
    sin_t = jnp.where(lane < DIFF_DH // 2, -1.0, 1.0) * (tsn * rc + tc * rsn)

    q_scale = (DIFF_DH ** -0.5) * LOG2E
    yq = _dot(hq, w_ref[:, 0:d])
    for gidx, yr in enumerate(_rope(yq, cos_t, sin_t)):
        o_ref[0, :, gidx * DIFF_DH:(gidx + 1) * DIFF_DH] = (yr * q_scale).astype(BF16)
    yk = _dot(hk, w_ref[:, d:2 * d])
    for gidx, yr in enumerate(_rope(yk, cos_t, sin_t)):
        o_ref[0, :, d + gidx * DIFF_DH:d + (gidx + 1) * DIFF_DH] = yr.astype(BF16)
    o_ref[0, :, 2 * d:3 * d] = _dot(hk, w_ref[:, 2 * d:3 * d]).astype(BF16)


def _attn_inproj(x, mod, kvmod, gq_row, gkv_row, w, tile_cs, row_cs, ts=TS_PROJ):
    bsz, seq, d = x.shape
    n_out = w.shape[1]
    const = lambda shape: pl.BlockSpec(shape, lambda b, s: (0,) * len(shape),
                                       pipeline_mode=pl.Buffered(1))
    return pl.pallas_call(
        _attn_inproj_kernel,
        grid=(bsz, seq // ts),
        in_specs=[
            pl.BlockSpec((1, ts, d), lambda b, s: (b, s, 0)),
            pl.BlockSpec((1, 6, d), lambda b, s: (b, 0, 0)),
            pl.BlockSpec((1, 2, d), lambda b, s: (b, 0, 0)),
            const((1, d)),
            const((1, d)),
            const((d, n_out)),
            pl.BlockSpec((1, 2, DIFF_DH), lambda b, s: (s, 0, 0)),
            const((2, ts, DIFF_DH)),
        ],
        out_specs=pl.BlockSpec((1, ts, n_out), lambda b, s: (b, s, 0)),
        out_shape=jax.ShapeDtypeStruct((bsz, seq, n_out), BF16),
        compiler_params=_params(("arbitrary", "arbitrary")),
        name="attn_inproj",
    )(x, mod, kvmod, gq_row, gkv_row, w, tile_cs, row_cs)


def _diff_attn_kernel(q_ref, k_ref, v_ref, lam_ref, g_ref, o_ref, m_ref, l_ref, acc_ref,
                      s_ref, mx_ref, *, tq, tk, lam_init):
    qi = pl.program_id(2)
    m_ref[...] = jnp.full(m_ref.shape, -jnp.inf, F32)
    l_ref[...] = jnp.zeros_like(l_ref)
    acc_ref[...] = jnp.zeros_like(acc_ref)

    rt = ATTN_ROW_TILE
    n_rt = tq // rt

    def chunks(width):
        return [slice(j * V7X_LANES, (j + 1) * V7X_LANES) for j in range(width // V7X_LANES)]

    def scores_rows(kblk, slot, rq, width=None):
        rk = pl.ds(pl.multiple_of(kblk * tk, tk), tk if width is None else width)
        for mi in range(2):
            cs = slice(mi * DIFF_DH, (mi + 1) * DIFF_DH)
            s = _dot_nt(q_ref[0, rq, cs], k_ref[0, rk, cs])
            if width is None:
                s_ref[slot, mi, rq, :] = s
                mx_ref[slot, mi, rq, :] = functools.reduce(jnp.maximum, [s[:, c] for c in chunks(tk)])
            else:
                s_ref[slot, mi, rq, 0:width] = s

    def softmax_rows(slot, rq, width, diag_row0):
        if diag_row0 is not None:
            r = lax.broadcasted_iota(jnp.int32, (rt, V7X_LANES), 0)
            c = lax.broadcasted_iota(jnp.int32, (rt, V7X_LANES), 1)
            diff = r - c + diag_row0
        out = []
        for mi in range(2):
            sc = [s_ref[slot, mi, rq, c_] for c_ in chunks(width)]
            if diag_row0 is not None:
                sc = [jnp.where(diff >= c_.start, x, -jnp.inf) for c_, x in zip(chunks(width), sc)]
                mx = functools.reduce(jnp.maximum, sc)
            else:
                mx = mx_ref[slot, mi, rq, :]
            m_old = m_ref[mi, rq, :]
            m_new = jnp.maximum(m_old, jnp.max(mx, axis=-1, keepdims=True))
            alpha = jnp.exp2(m_old - m_new)
            m_ref[mi, rq, :] = m_new
            ps = [jnp.exp2(x - m_new) for x in sc]
            l_ref[mi, rq, :] = alpha * l_ref[mi, rq, :] + functools.reduce(jnp.add, ps)
            out.append((alpha, jnp.concatenate(ps, axis=1).astype(BF16)))
        return out

    def pv_rows(mi, rq, alpha, p, v):
        acc_ref[mi, rq, :] = jnp.concatenate([alpha, alpha], axis=1) * acc_ref[mi, rq, :] + _dot(p, v)

    def row_tile(t):
        return pl.ds(pl.multiple_of(t * rt, rt), rt)

    blocks_per_k = tk // tq
    n_full = qi // blocks_per_k
    rem = qi - n_full * blocks_per_k

    kdiag = pl.multiple_of(n_full * tk, tk)
    for rem_static in range(blocks_per_k):
        @pl.when(rem == rem_static)
        def _():
            for t in range(n_rt):
                scores_rows(n_full, 0, pl.ds(t * rt, rt), width=rem_static * tq + (t + 1) * rt)
            for t in range(n_rt):
                rq = pl.ds(t * rt, rt)
                first_visible = rem_static * tq + t * rt
                width = first_visible + rt
                v = v_ref[0, pl.ds(kdiag, width), :]
                for mi, (alpha, p) in enumerate(softmax_rows(0, rq, width, first_visible)):
                    pv_rows(mi, rq, alpha, p, v)
                scores_rows(0, 1, rq)

    def visible_block(i, produce_next):
        slot = (i + 1) & 1
        v = v_ref[0, pl.ds(pl.multiple_of(i * tk, tk), tk), :]

        def tile(t, c):
            rq = row_tile(t)
            for mi, (alpha, p) in enumerate(softmax_rows(slot, rq, tk, None)):
                pv_rows(mi, rq, alpha, p, v)
            if produce_next:
                scores_rows(i + 1, 1 - slot, rq)
            return c

        lax.fori_loop(0, n_rt, tile, 0)

    def visible_step(i, carry):
        visible_block(i, True)
        return carry

    lax.fori_loop(0, n_full - 1, visible_step, 0)

    @pl.when(n_full > 0)
    def _():
        visible_block(n_full - 1, False)

    lp = lam_ref[...]
    lam = (jnp.exp(jnp.sum(lp[0:1] * lp[1:2], axis=-1, keepdims=True))
           - jnp.exp(jnp.sum(lp[2:3] * lp[3:4], axis=-1, keepdims=True)) + lam_init)
    l0 = jnp.sum(l_ref[0], axis=-1, keepdims=True)
    l1 = jnp.sum(l_ref[1], axis=-1, keepdims=True)
    o = acc_ref[0] / l0 - lam * (acc_ref[1] / l1)
    on = o * lax.rsqrt(jnp.mean(o * o, axis=-1, keepdims=True) + EPS) * g_ref[...]
    o_ref[0] = (on * (1.0 - lam_init)).astype(BF16)


def _diff_attn(qkv, lam_params, g_row, lam_init, tq=TQ_ATTN, tk=TK_ATTN):
    bsz, seq, _ = qkv.shape
    hw = 2 * DIFF_DH
    assert tk % tq == 0 and seq % tk == 0
    return pl.pallas_call(
        functools.partial(_diff_attn_kernel, tq=tq, tk=tk, lam_init=lam_init),
        grid=(bsz, DIFF_HEADS, seq // tq),
        in_specs=[
            pl.BlockSpec((1, tq, hw), lambda b, h, i: (b, i, h)),
            pl.BlockSpec((1, seq, hw), lambda b, h, i: (b, 0, DIFF_HEADS + h),
                         pipeline_mode=pl.Buffered(1)),
            pl.BlockSpec((1, seq, hw), lambda b, h, i: (b, 0, 2 * DIFF_HEADS + h),
                         pipeline_mode=pl.Buffered(1)),
            pl.BlockSpec((4, DIFF_DH), lambda b, h, i: (0, 0)),
            pl.BlockSpec((1, hw), lambda b, h, i: (0, 0)),
        ],
        out_specs=pl.BlockSpec((1, tq, hw), lambda b, h, i: (b, i, h)),
        out_shape=jax.ShapeDtypeStruct((bsz, seq, DIFF_HEADS * hw), BF16),
        scratch_shapes=[
            pltpu.VMEM((2, tq, V7X_LANES), F32),
            pltpu.VMEM((2, tq, V7X_LANES), F32),
            pltpu.VMEM((2, tq, hw), F32),
            pltpu.VMEM((2, 2, tq, tk), F32),
            pltpu.VMEM((2, 2, tq, V7X_LANES), F32),
        ],
        compiler_params=_params(("arbitrary", "arbitrary", "arbitrary")),
        name="diff_attn",
    )(qkv, qkv, qkv, lam_params, g_row)


def _rope_tables(seq, ts):
    inv_freq = ROPE_THETA ** (-jnp.arange(0, ROT_DIM, 2, dtype=F32) / ROT_DIM)
    pad = (DIFF_DH - ROT_DIM) // 2

    def lanes(pos):
        freqs = pos[:, None] * inv_freq[None, :]
        cos, sin = jnp.cos(freqs), jnp.sin(freqs)
        ones, zeros = jnp.ones((pos.shape[0], pad), F32), jnp.zeros((pos.shape[0], pad), F32)
        return (jnp.concatenate([cos, ones, cos, ones], axis=1),
                jnp.concatenate([sin, zeros, sin, zeros], axis=1))

    tile_cs = jnp.stack(lanes(jnp.arange(seq // ts, dtype=F32) * ts), axis=1)
    row_cs = jnp.stack(lanes(jnp.arange(ts, dtype=F32)), axis=0)
    return tile_cs, row_cs


def kernel(x, c, mod_w, mod_b, norm_mix_g, norm_mlp_g, a_w_in, a_conv_w, a_log, a_dt_bias, a_out_norm_g, a_w_out,
           kv_mod_w, kv_mod_b, kv_norm_g, kv_w, b_w_q, b_lambda, b_subln_g, b_w_out, mlp_w1, mlp_w2, final_g):
    bsz, seq, d = x.shape
    depth = mod_w.shape[0]
    assert depth == 2 and a_w_in.shape[0] == 1 and b_w_q.shape[0] == 1
    assert seq % TS_PROJ == 0 and seq % TS_SCAN == 0 and seq % TM_MLP == 0 and seq % TQ_ATTN == 0

    c_t = c.T
    mod = _mod_call(c_t, mod_w, mod_b[:, None, :]).reshape(depth, bsz, 6, d)
    kvmod = _mod_call(c_t, kv_mod_w[None], kv_mod_b[None, None, :]).reshape(bsz, 2, d)

    row = lambda v: v.reshape(1, -1).astype(F32)

    w_in = a_w_in[0]
    n_main = 4 * d
    w_main = w_in[:, :n_main].astype(BF16)
    w_ab = jnp.pad(w_in[:, n_main:], ((0, 0), (0, V7X_LANES - 2 * GDN_HEADS))).astype(BF16)
    ad = jnp.pad(jnp.stack([a_log[0], a_dt_bias[0]]).astype(F32), ((0, 0), (0, V7X_LANES - GDN_HEADS)))
    qkvz, gb = _gdn_inproj(x, mod[0], row(norm_mix_g[0]), w_main, w_ab, a_conv_w[0].astype(F32), ad)
    y0 = _gdn_scan(qkvz, gb, row(a_out_norm_g[0]))
    x = _mixer_out_mlp(x, y0, mod[0], a_w_out[0].astype(BF16), row(norm_mlp_g[0]),
                       mlp_w1[0].astype(BF16), mlp_w2[0].astype(BF16), row(final_g), False)

    lam_init = 0.8 - 0.6 * math.exp(-0.3 * 1)
    def rope_order(w):
        half, mid = ROT_DIM // 2, DIFF_DH // 2
        g = w.reshape(d, -1, DIFF_DH)
        g = jnp.concatenate([g[..., :half], g[..., ROT_DIM:mid + half], g[..., half:ROT_DIM],
                             g[..., mid + half:]], axis=-1)
        return g.reshape(w.shape)

    n_k = 2 * DIFF_HEADS * DIFF_DH
    w_qkv = jnp.concatenate([rope_order(b_w_q[0]), rope_order(kv_w[:, :n_k]), kv_w[:, n_k:]],
                            axis=1).astype(BF16)
    tile_cs, row_cs = _rope_tables(seq, TS_PROJ)
    qkv = _attn_inproj(x, mod[1], kvmod, row(norm_mix_g[1]), row(kv_norm_g), w_qkv, tile_cs, row_cs)
    y1 = _diff_attn(qkv, b_lambda[0].astype(F32), row(b_subln_g[0]), lam_init)
    x = _mixer_out_mlp(x, y1, mod[1], b_w_out[0].astype(BF16), row(norm_mlp_g[1]),
                       mlp_w1[1].astype(BF16), mlp_w2[1].astype(BF16), row(final_g), True)
    return x
```

```python
import functools
import math

import jax
import jax.numpy as jnp
from jax import lax
from jax.experimental import pallas as pl
from jax.experimental.pallas import tpu as pltpu

F32 = jnp.float32
BF16 = jnp.bfloat16
HIGHEST = lax.Precision.HIGHEST

V7X_LANES = 128
V7X_SUBLANES = 8
V7X_VMEM_LIMIT_BYTES = 56 * 1024 * 1024

EPS = 1e-6
GDN_HEADS = 8
GDN_DK = 128
GDN_CHUNK = 64
GDN_CHUNKS_PER_STEP = 2
CONV_K = 4
DIFF_HEADS = 4
DIFF_DH = 128
ROT_DIM = DIFF_DH // 4
ROPE_THETA = 500000.0
LOG2E = 1.4426950408889634

TS_PROJ = 512
TS_SCAN = 512
TM_MLP = 1024
TQ_ATTN = 512
TK_ATTN = 2048
ATTN_ROW_TILE = 256
FF_CHUNK = 1024


def _params(sem, vmem=V7X_VMEM_LIMIT_BYTES, flags=None):
    return pltpu.CompilerParams(dimension_semantics=sem, vmem_limit_bytes=vmem, flags=flags)


def _sigmoid(x):
    return 1.0 / (1.0 + jnp.exp(-x))


def _silu(x):
    return x * _sigmoid(x)


def _softplus(x):
    return jnp.maximum(x, 0.0) + jnp.log1p(jnp.exp(-jnp.abs(x)))


def _dot(a, b, precision=None):
    return jnp.dot(a, b, preferred_element_type=F32, precision=precision)


def _dot_nt(a, b):
    return lax.dot_general(a, b, (((1,), (1,)), ((), ())), preferred_element_type=F32)


def _dot_tn(a, b):
    return lax.dot_general(a, b, (((0,), (0,)), ((), ())), preferred_element_type=F32)


def _split_bf16(x):
    hi = x.astype(BF16)
    return hi, (x - hi.astype(F32)).astype(BF16)


def _dot3(a, b):
    a_hi, a_lo = _split_bf16(a)
    b_hi, b_lo = _split_bf16(b)
    return _dot(jnp.concatenate([a_hi, a_lo, a_hi], axis=1),
                jnp.concatenate([b_hi, b_hi, b_lo], axis=0))


def _mod_kernel(ct_ref, w_ref, b_ref, o_ref, *, batch):
    cs = _silu(ct_ref[...])
    w = w_ref[0]
    for b in range(batch):
        o_ref[0, b:b + 1, :] = jnp.sum(w * cs[:, b:b + 1], axis=0, keepdims=True) + b_ref[0]


def _mod_call(c_t, w, bias, tn=2048):
    n_l, d, n = w.shape
    batch = c_t.shape[1]
    return pl.pallas_call(
        functools.partial(_mod_kernel, batch=batch),
        grid=(n_l, n // tn),
        in_specs=[
            pl.BlockSpec((d, batch), lambda l, j: (0, 0)),
            pl.BlockSpec((1, d, tn), lambda l, j: (l, 0, j)),
            pl.BlockSpec((1, 1, tn), lambda l, j: (l, 0, j)),
        ],
        out_specs=pl.BlockSpec((1, batch, tn), lambda l, j: (l, 0, j)),
        out_shape=jax.ShapeDtypeStruct((n_l, batch, n), F32),
        compiler_params=_params(("arbitrary", "arbitrary")),
        name="mod_vectors",
    )(c_t, w, bias)


def _modulated_norm(x, g_row, shift_row, scale_row):
    rstd = lax.rsqrt(jnp.mean(x * x, axis=-1, keepdims=True) + EPS)
    return (x * rstd) * (g_row * (1.0 + scale_row)) + shift_row


def _gdn_inproj_kernel(x_ref, mod_ref, g_ref, w_ref, wab_ref, cw_ref, ad_ref,
                       o_ref, gb_ref, big_ref, *, ts):
    d = x_ref.shape[2]

    @pl.when(pl.program_id(1) == 0)
    def _():
        big_ref[:, ts:ts + V7X_SUBLANES] = jnp.zeros((3, V7X_SUBLANES, d), F32)

    m = mod_ref[0]
    hb = _modulated_norm(x_ref[0], g_ref[...], m[0:1], m[1:2]).astype(BF16)
    ab = _dot(hb, wab_ref[...])
    ad = ad_ref[...]
    g = -jnp.exp(ad[0:1]) * _softplus(ab + ad[1:2])
    lane = lax.broadcasted_iota(jnp.int32, ab.shape, 1)
    gb_ref[0] = jnp.where(lane < GDN_HEADS, g, _sigmoid(ab))

    base = V7X_SUBLANES - (CONV_K - 1)
    for j in range(3):
        cs_j = slice(j * d, (j + 1) * d)
        proj = _dot(hb, w_ref[:, cs_j])
        big_ref[j, 0:V7X_SUBLANES] = big_ref[j, ts:ts + V7X_SUBLANES]
        big_ref[j, V7X_SUBLANES:] = proj
        cw = cw_ref[:, cs_j]
        y = big_ref[j, base:base + ts] * cw[0:1]
        for t in range(1, CONV_K):
            y = y + big_ref[j, base + t:base + t + ts] * cw[t:t + 1]
        y = _silu(y)
        if j == 2:
            o_ref[0, :, cs_j] = y
        else:
            for hh in range(GDN_HEADS):
                cs = slice(hh * GDN_DK, (hh + 1) * GDN_DK)
                yh = y[:, cs]
                o_ref[0, :, j * d + hh * GDN_DK:j * d + (hh + 1) * GDN_DK] = (
                    yh * lax.rsqrt(jnp.sum(yh * yh, axis=-1, keepdims=True) + EPS))
    o_ref[0, :, 3 * d:4 * d] = _dot(hb, w_ref[:, 3 * d:4 * d])


def _gdn_inproj(x, mod, g_row, w, wab, cw, ad, ts=TS_PROJ):
    bsz, seq, d = x.shape
    n_out = w.shape[1]
    const = lambda shape: pl.BlockSpec(shape, lambda b, s: (0,) * len(shape),
                                       pipeline_mode=pl.Buffered(1))
    return pl.pallas_call(
        functools.partial(_gdn_inproj_kernel, ts=ts),
        grid=(bsz, seq // ts),
        in_specs=[
            pl.BlockSpec((1, ts, d), lambda b, s: (b, s, 0)),
            pl.BlockSpec((1, 6, d), lambda b, s: (b, 0, 0)),
            const((1, d)),
            const((d, n_out)),
            const((d, V7X_LANES)),
            const((CONV_K, 3 * d)),
            const((2, V7X_LANES)),
        ],
        out_specs=[
            pl.BlockSpec((1, ts, n_out), lambda b, s: (b, s, 0)),
            pl.BlockSpec((1, ts, V7X_LANES), lambda b, s: (b, s, 0)),
        ],
        out_shape=[
            jax.ShapeDtypeStruct((bsz, seq, n_out), F32),
            jax.ShapeDtypeStruct((bsz, seq, V7X_LANES), F32),
        ],
        scratch_shapes=[pltpu.VMEM((3, ts + V7X_SUBLANES, d), F32)],
        compiler_params=_params(("arbitrary", "arbitrary")),
        name="gdn_inproj",
    )(x, mod, g_row, w, wab, cw, ad)


def _gdn_scan_kernel(q_ref, k_ref, v_ref, z_ref, gb_ref, og_ref, y_ref, state_ref, *, ts):
    c_sz = GDN_CHUNK

    @pl.when(pl.program_id(1) == 0)
    def _():
        state_ref[...] = jnp.zeros_like(state_ref)

    row = lax.broadcasted_iota(jnp.int32, (c_sz, c_sz), 0)
    col = lax.broadcasted_iota(jnp.int32, (c_sz, c_sz), 1)
    incl = row >= col
    strict = row > col
    ltri = incl.astype(F32)
    eye = (row == col).astype(F32)
    og = og_ref[...]
    q_scale = GDN_DK ** -0.5

    heads = range(GDN_HEADS)
    cols = [slice(h * GDN_DK, (h + 1) * GDN_DK) for h in heads]
    n_sub = GDN_CHUNKS_PER_STEP

    def step(c, carry):
        rows, gbc, gcum, gcum_t = [], [], [], []
        for ci in range(n_sub):
            r0 = pl.multiple_of((c * n_sub + ci) * c_sz, c_sz)
            rows.append(pl.ds(r0, c_sz))
            gbc.append(gb_ref[0, rows[ci], :])
            gcum.append(_dot(ltri, gbc[ci], HIGHEST))
            gcum_t.append(gcum[ci].T)
        streams = [(ci, h) for ci in range(n_sub) for h in heads]
        qh = {s: q_ref[0, rows[s[0]], cols[s[1]]] * q_scale for s in streams}
        kh = {s: k_ref[0, rows[s[0]], cols[s[1]]] for s in streams}
        gc = {(ci, h): gcum[ci][:, h:h + 1] for ci, h in streams}
        beta = {(ci, h): gbc[ci][:, GDN_HEADS + h:GDN_HEADS + h + 1] for ci, h in streams}
        glast = {(ci, h): gcum[ci][c_sz - 1:c_sz, h:h + 1] for ci, h in streams}
        eg = {s: jnp.exp(gc[s]) for s in streams}
        decay = {(ci, h): jnp.where(incl, jnp.exp(jnp.where(incl, gc[(ci, h)] - gcum_t[ci][h:h + 1, :],
                                                           0.0)), 0.0) for ci, h in streams}
        kb = {s: kh[s] * beta[s] for s in streams}
        kk = {s: _dot_nt(jnp.concatenate([kb[s], qh[s]], axis=0).astype(BF16), kh[s].astype(BF16))
              for s in streams}
        a_mat = {s: jnp.where(strict, kk[s][:c_sz] * decay[s], 0.0) for s in streams}
        qk = {s: (kk[s][c_sz:] * decay[s]).astype(BF16) for s in streams}
        x_mat = {s: eye - a_mat[s] for s in streams}
        p_mat = {s: _dot3(a_mat[s], a_mat[s]) for s in streams}
        for _ in range(4):
            xp = {s: _dot3(jnp.concatenate([x_mat[s], p_mat[s]], axis=0), p_mat[s]) for s in streams}
            x_mat = {s: x_mat[s] + xp[s][:c_sz] for s in streams}
            p_mat = {s: xp[s][c_sz:] for s in streams}
        xl = {s: _dot3(x_mat[s], p_mat[s]) for s in streams}
        x_mat = {s: x_mat[s] + xl[s] for s in streams}
        sol = {s: _dot3(x_mat[s], jnp.concatenate([v_ref[0, rows[s[0]], cols[s[1]]] * beta[s],
                                                   kb[s] * eg[s]], axis=1))
               for s in streams}
        wq_lhs = {s: jnp.concatenate([sol[s][:, GDN_DK:], qh[s] * eg[s]], axis=0).astype(BF16)
                  for s in streams}
        kdec = {s: (kh[s] * jnp.exp(glast[s] - gc[s])).astype(BF16) for s in streams}
        for ci in range(n_sub):
            st = [state_ref[h] for h in heads]
            wq = [_dot(wq_lhs[(ci, h)], st[h].astype(BF16)) for h in heads]
            v_new = [(sol[(ci, h)][:, :GDN_DK] - wq[h][:c_sz]).astype(BF16) for h in heads]
            o = [wq[h][c_sz:] + _dot(qk[(ci, h)], v_new[h]) for h in heads]
            for h in heads:
                state_ref[h] = st[h] * jnp.exp(glast[(ci, h)]) + _dot_tn(kdec[(ci, h)], v_new[h])
            for h in heads:
                on = o[h] * lax.rsqrt(jnp.mean(o[h] * o[h], axis=-1, keepdims=True) + EPS) * og
                y_ref[0, rows[ci], cols[h]] = (on * _silu(z_ref[0, rows[ci], cols[h]])).astype(BF16)
        return carry

    lax.fori_loop(0, ts // (c_sz * n_sub), step, 0)


def _gdn_scan(qkvz, gb, og_row, ts=TS_SCAN):
    bsz, seq, _ = qkvz.shape
    d = GDN_HEADS * GDN_DK
    col = lambda j: pl.BlockSpec((1, ts, d), lambda b, s: (b, s, j))
    return pl.pallas_call(
        functools.partial(_gdn_scan_kernel, ts=ts),
        grid=(bsz, seq // ts),
        in_specs=[col(0), col(1), col(2), col(3),
                  pl.BlockSpec((1, ts, V7X_LANES), lambda b, s: (b, s, 0)),
                  pl.BlockSpec((1, GDN_DK), lambda b, s: (0, 0))],
        out_specs=pl.BlockSpec((1, ts, d), lambda b, s: (b, s, 0)),
        out_shape=jax.ShapeDtypeStruct((bsz, seq, d), BF16),
        scratch_shapes=[pltpu.VMEM((GDN_HEADS, GDN_DK, GDN_DK), F32)],
        compiler_params=_params(("arbitrary", "arbitrary")),
        name="gdn_scan",
    )(qkvz, qkvz, qkvz, qkvz, gb, og_row)


def _mixer_out_mlp_kernel(x_ref, y_ref, mod_ref, wo_ref, g_ref, w1_ref, w2_ref, fg_ref, o_ref,
                          *, final_norm):
    m = mod_ref[0]
    x1 = x_ref[0] + m[2:3] * _dot(y_ref[0], wo_ref[...])
    h = _modulated_norm(x1, g_ref[...], m[3:4], m[4:5]).astype(BF16)
    d_ff = w1_ref.shape[1]
    acc = None
    for c in range(d_ff // FF_CHUNK):
        cs = slice(c * FF_CHUNK, (c + 1) * FF_CHUNK)
        hid = jnp.maximum(_dot(h, w1_ref[:, cs]), 0.0)
        part = _dot((hid * hid).astype(BF16), w2_ref[cs, :])
        acc = part if acc is None else acc + part
    x2 = x1 + m[5:6] * acc
    if final_norm:
        x2 = x2 * lax.rsqrt(jnp.mean(x2 * x2, axis=-1, keepdims=True) + EPS) * fg_ref[...]
    o_ref[0] = x2


def _mixer_out_mlp(x, y, mod, wo, g_row, w1, w2, fg_row, final_norm, tm=TM_MLP):
    bsz, seq, d = x.shape
    d_ff = w1.shape[1]
    const = lambda shape: pl.BlockSpec(shape, lambda b, s: (0,) * len(shape),
                                       pipeline_mode=pl.Buffered(1))
    return pl.pallas_call(
        functools.partial(_mixer_out_mlp_kernel, final_norm=final_norm),
        grid=(bsz, seq // tm),
        in_specs=[
            pl.BlockSpec((1, tm, d), lambda b, s: (b, s, 0)),
            pl.BlockSpec((1, tm, d), lambda b, s: (b, s, 0)),
            pl.BlockSpec((1, 6, d), lambda b, s: (b, 0, 0)),
            const((d, d)),
            const((1, d)),
            const((d, d_ff)),
            const((d_ff, d)),
            const((1, d)),
        ],
        out_specs=pl.BlockSpec((1, tm, d), lambda b, s: (b, s, 0)),
        out_shape=jax.ShapeDtypeStruct((bsz, seq, d), F32),
        compiler_params=_params(("arbitrary", "arbitrary")),
        name="mixer_out_mlp",
    )(x, y, mod, wo, g_row, w1, w2, fg_row)


def _rope(y, cos_t, sin_t):
    outs = []
    for gidx in range(y.shape[1] // DIFF_DH):
        yh = y[:, gidx * DIFF_DH:(gidx + 1) * DIFF_DH]
        outs.append(yh * cos_t + pltpu.roll(yh, DIFF_DH // 2, 1) * sin_t)
    return outs


def _attn_inproj_kernel(x_ref, mod_ref, kvmod_ref, gq_ref, gkv_ref, w_ref, tile_cs_ref, row_cs_ref,
                        o_ref):
    d = x_ref.shape[2]
    x = x_ref[0]
    xn = x * lax.rsqrt(jnp.mean(x * x, axis=-1, keepdims=True) + EPS)
    m = mod_ref[0]
    km = kvmod_ref[0]
    hq = (xn * (gq_ref[...] * (1.0 + m[1:2])) + m[0:1]).astype(BF16)
    hk = (xn * (gkv_ref[...] * (1.0 + km[1:2])) + km[0:1]).astype(BF16)

    tc, tsn = tile_cs_ref[0, 0:1, :], tile_cs_ref[0, 1:2, :]
    rc, rsn = row_cs_ref[0], row_cs_ref[1]
    lane = lax.broadcasted_iota(jnp.int32, rc.shape, 1)
    cos_t = tc * rc - tsn * rsn
    sin_t = jnp.where(lane < DIFF_DH // 2, -1.0, 1.0) * (tsn * rc + tc * rsn)

    q_scale = (DIFF_DH ** -0.5) * LOG2E
    yq = _dot(hq, w_ref[:, 0:d])
    for gidx, yr in enumerate(_rope(yq, cos_t, sin_t)):
        o_ref[0, :, gidx * DIFF_DH:(gidx + 1) * DIFF_DH] = (yr * q_scale).astype(BF16)
    yk = _dot(hk, w_ref[:, d:2 * d])
    for gidx, yr in enumerate(_rope(yk, cos_t, sin_t)):
        o_ref[0, :, d + gidx * DIFF_DH:d + (gidx + 1) * DIFF_DH] = yr.astype(BF16)
    o_ref[0, :, 2 * d:3 * d] = _dot(hk, w_ref[:, 2 * d:3 * d]).astype(BF16)


def _attn_inproj(x, mod, kvmod, gq_row, gkv_row, w, tile_cs, row_cs, ts=TS_PROJ):
    bsz, seq, d = x.shape
    n_out = w.shape[1]
    const = lambda shape: pl.BlockSpec(shape, lambda b, s: (0,) * len(shape),
                                       pipeline_mode=pl.Buffered(1))
    return pl.pallas_call(
        _attn_inproj_kernel,
        grid=(bsz, seq // ts),
        in_specs=[
            pl.BlockSpec((1, ts, d), lambda b, s: (b, s, 0)),
            pl.BlockSpec((1, 6, d), lambda b, s: (b, 0, 0)),
            pl.BlockSpec((1, 2, d), lambda b, s: (b, 0, 0)),
            const((1, d)),
            const((1, d)),
            const((d, n_out)),
            pl.BlockSpec((1, 2, DIFF_DH), lambda b, s: (s, 0, 0)),
            const((2, ts, DIFF_DH)),
        ],
        out_specs=pl.BlockSpec((1, ts, n_out), lambda b, s: (b, s, 0)),
        out_shape=jax.ShapeDtypeStruct((bsz, seq, n_out), BF16),
        compiler_params=_params(("arbitrary", "arbitrary")),
        name="attn_inproj",
    )(x, mod, kvmod, gq_row, gkv_row, w, tile_cs, row_cs)


def _diff_attn_kernel(q_ref, k_ref, v_ref, lam_ref, g_ref, o_ref, m_ref, l_ref, acc_ref,
                      s_ref, mx_ref, *, tq, tk, lam_init):
    qi = pl.program_id(2)
    m_ref[...] = jnp.full(m_ref.shape, -jnp.inf, F32)
    l_ref[...] = jnp.zeros_like(l_ref)
    acc_ref[...] = jnp.zeros_like(acc_ref)

    rt = ATTN_ROW_TILE
    n_rt = tq // rt

    def chunks(width):
        return [slice(j * V7X_LANES, (j + 1) * V7X_LANES) for j in range(width // V7X_LANES)]

    def scores_rows(kblk, slot, rq, width=None):
        rk = pl.ds(pl.multiple_of(kblk * tk, tk), tk if width is None else width)
        for mi in range(2):
            cs = slice(mi * DIFF_DH, (mi + 1) * DIFF_DH)
            s = _dot_nt(q_ref[0, rq, cs], k_ref[0, rk, cs])
            if width is None:
                s_ref[slot, mi, rq, :] = s
                mx_ref[slot, mi, rq, :] = functools.reduce(jnp.maximum, [s[:, c] for c in chunks(tk)])
            else:
                s_ref[slot, mi, rq, 0:width] = s

    def softmax_rows(slot, rq, width, diag_row0):
        if diag_row0 is not None:
            r = lax.broadcasted_iota(jnp.int32, (rt, V7X_LANES), 0)
            c = lax.broadcasted_iota(jnp.int32, (rt, V7X_LANES), 1)
            diff = r - c + diag_row0
        out = []
        for mi in range(2):
            sc = [s_ref[slot, mi, rq, c_] for c_ in chunks(width)]
            if diag_row0 is not None:
                sc = [jnp.where(diff >= c_.start, x, -jnp.inf) for c_, x in zip(chunks(width), sc)]
                mx = functools.reduce(jnp.maximum, sc)
            else:
                mx = mx_ref[slot, mi, rq, :]
            m_old = m_ref[mi, rq, :]
            m_new = jnp.maximum(m_old, jnp.max(mx, axis=-1, keepdims=True))
            alpha = jnp.exp2(m_old - m_new)
            m_ref[mi, rq, :] = m_new
            ps = [jnp.exp2(x - m_new) for x in sc]
            l_ref[mi, rq, :] = alpha * l_ref[mi, rq, :] + functools.reduce(jnp.add, ps)
            out.append((alpha, jnp.concatenate(ps, axis=1).astype(BF16)))
        return out

    def pv_rows(mi, rq, alpha, p, v):
        acc_ref[mi, rq, :] = jnp.concatenate([alpha, alpha], axis=1) * acc_ref[mi, rq, :] + _dot(p, v)

    def row_tile(t):
        return pl.ds(pl.multiple_of(t * rt, rt), rt)

    blocks_per_k = tk // tq
    n_full = qi // blocks_per_k
    rem = qi - n_full * blocks_per_k

    kdiag = pl.multiple_of(n_full * tk, tk)
    for rem_static in range(blocks_per_k):
        @pl.when(rem == rem_static)
        def _():
            for t in range(n_rt):
                scores_rows(n_full, 0, pl.ds(t * rt, rt), width=rem_static * tq + (t + 1) * rt)
            for t in range(n_rt):
                rq = pl.ds(t * rt, rt)
                first_visible = rem_static * tq + t * rt
                width = first_visible + rt
                v = v_ref[0, pl.ds(kdiag, width), :]
                for mi, (alpha, p) in enumerate(softmax_rows(0, rq, width, first_visible)):
                    pv_rows(mi, rq, alpha, p, v)
                scores_rows(0, 1, rq)

    def visible_block(i, produce_next):
        slot = (i + 1) & 1
        v = v_ref[0, pl.ds(pl.multiple_of(i * tk, tk), tk), :]

        def tile(t, c):
            rq = row_tile(t)
            for mi, (alpha, p) in enumerate(softmax_rows(slot, rq, tk, None)):
                pv_rows(mi, rq, alpha, p, v)
            if produce_next:
                scores_rows(i + 1, 1 - slot, rq)
            return c

        lax.fori_loop(0, n_rt, tile, 0)

    def visible_step(i, carry):
        visible_block(i, True)
        return carry

    lax.fori_loop(0, n_full - 1, visible_step, 0)

    @pl.when(n_full > 0)
    def _():
        visible_block(n_full - 1, False)

    lp = lam_ref[...]
    lam = (jnp.exp(jnp.sum(lp[0:1] * lp[1:2], axis=-1, keepdims=True))
           - jnp.exp(jnp.sum(lp[2:3] * lp[3:4], axis=-1, keepdims=True)) + lam_init)
    l0 = jnp.sum(l_ref[0], axis=-1, keepdims=True)
    l1 = jnp.sum(l_ref[1], axis=-1, keepdims=True)
    o = acc_ref[0] / l0 - lam * (acc_ref[1] / l1)
    on = o * lax.rsqrt(jnp.mean(o * o, axis=-1, keepdims=True) + EPS) * g_ref[...]
    o_ref[0] = (on * (1.0 - lam_init)).astype(BF16)


def _diff_attn(qkv, lam_params, g_row, lam_init, tq=TQ_ATTN, tk=TK_ATTN):
    bsz, seq, _ = qkv.shape
    hw = 2 * DIFF_DH
    assert tk % tq == 0 and seq % tk == 0
    return pl.pallas_call(
        functools.partial(_diff_attn_kernel, tq=tq, tk=tk, lam_init=lam_init),
        grid=(bsz, DIFF_HEADS, seq // tq),
        in_specs=[
            pl.BlockSpec((1, tq, hw), lambda b, h, i: (b, i, h)),
            pl.BlockSpec((1, seq, hw), lambda b, h, i: (b, 0, DIFF_HEADS + h)),
            pl.BlockSpec((1, seq, hw), lambda b, h, i: (b, 0, 2 * DIFF_HEADS + h),
                         pipeline_mode=pl.Buffered(1)),
            pl.BlockSpec((4, DIFF_DH), lambda b, h, i: (0, 0)),
            pl.BlockSpec((1, hw), lambda b, h, i: (0, 0)),
        ],
        out_specs=pl.BlockSpec((1, tq, hw), lambda b, h, i: (b, i, h)),
        out_shape=jax.ShapeDtypeStruct((bsz, seq, DIFF_HEADS * hw), BF16),
        scratch_shapes=[
            pltpu.VMEM((2, tq, V7X_LANES), F32),
            pltpu.VMEM((2, tq, V7X_LANES), F32),
            pltpu.VMEM((2, tq, hw), F32),
            pltpu.VMEM((2, 2, tq, tk), F32),
            pltpu.VMEM((2, 2, tq, V7X_LANES), F32),
        ],
        compiler_params=_params(("arbitrary", "arbitrary", "arbitrary")),
        name="diff_attn",
    )(qkv, qkv, qkv, lam_params, g_row)


def _rope_tables(seq, ts):
    inv_freq = ROPE_THETA ** (-jnp.arange(0, ROT_DIM, 2, dtype=F32) / ROT_DIM)
    pad = (DIFF_DH - ROT_DIM) // 2

    def lanes(pos):
        freqs = pos[:, None] * inv_freq[None, :]
        cos, sin = jnp.cos(freqs), jnp.sin(freqs)
        ones, zeros = jnp.ones((pos.shape[0], pad), F32), jnp.zeros((pos.shape[0], pad), F32)
        return (jnp.concatenate([cos, ones, cos, ones], axis=1),
                jnp.concatenate([sin, zeros, sin, zeros], axis=1))

    tile_cs = jnp.stack(lanes(jnp.arange(seq // ts, dtype=F32) * ts), axis=1)
    row_cs = jnp.stack(lanes(jnp.arange(ts, dtype=F32)), axis=0)
    return tile_cs, row_cs


def kernel(x, c, mod_w, mod_b, norm_mix_g, norm_mlp_g, a_w_in, a_conv_w, a_log, a_dt_bias, a_out_norm_g, a_w_out,
           kv_mod_w, kv_mod_b, kv_norm_g, kv_w, b_w_q, b_lambda, b_subln_g, b_w_out, mlp_w1, mlp_w2, final_g):
    bsz, seq, d = x.shape
    depth = mod_w.shape[0]
    assert depth == 2 and a_w_in.shape[0] == 1 and b_w_q.shape[0] == 1
    assert seq % TS_PROJ == 0 and seq % TS_SCAN == 0 and seq % TM_MLP == 0 and seq % TQ_ATTN == 0

    c_t = c.T
    mod = _mod_call(c_t, mod_w, mod_b[:, None, :]).reshape(depth, bsz, 6, d)
    kvmod = _mod_call(c_t, kv_mod_w[None], kv_mod_b[None, None, :]).reshape(bsz, 2, d)

    row = lambda v: v.reshape(1, -1).astype(F32)

    w_in = a_w_in[0]
    n_main = 4 * d
    w_main = w_in[:, :n_main].astype(BF16)
    w_ab = jnp.pad(w_in[:, n_main:], ((0, 0), (0, V7X_LANES - 2 * GDN_HEADS))).astype(BF16)
    ad = jnp.pad(jnp.stack([a_log[0], a_dt_bias[0]]).astype(F32), ((0, 0), (0, V7X_LANES - GDN_HEADS)))
    qkvz, gb = _gdn_inproj(x, mod[0], row(norm_mix_g[0]), w_main, w_ab, a_conv_w[0].astype(F32), ad)
    y0 = _gdn_scan(qkvz, gb, row(a_out_norm_g[0]))
    x = _mixer_out_mlp(x, y0, mod[0], a_w_out[0].astype(BF16), row(norm_mlp_g[0]),
                       mlp_w1[0].astype(BF16), mlp_w2[0].astype(BF16), row(final_g), False)

    lam_init = 0.8 - 0.6 * math.exp(-0.3 * 1)

    def rope_order(w):
        half, mid = ROT_DIM // 2, DIFF_DH // 2
        g = w.reshape(d, -1, DIFF_DH)
        g = jnp.concatenate([g[..., :half], g[..., ROT_DIM:mid + half], g[..., half:ROT_DIM],
                             g[..., mid + half:]], axis=-1)
        return g.reshape(w.shape)

    n_k = 2 * DIFF_HEADS * DIFF_DH
    w_qkv = jnp.concatenate([rope_order(b_w_q[0]), rope_order(kv_w[:, :n_k]), kv_w[:, n_k:]],
                            axis=1).astype(BF16)
    tile_cs, row_cs = _rope_tables(seq, TS_PROJ)
    qkv = _attn_inproj(x, mod[1], kvmod, row(norm_mix_g[1]), row(kv_norm_g), w_qkv, tile_cs, row_cs)
    y1 = _diff_attn(qkv, b_lambda[0].astype(F32), row(b_subln_g[0]), lam_init)
    x = _mixer_out_mlp(x, y1, mod[1], b_w_out[0].astype(BF16), row(norm_mlp_g[1]),
                       mlp_w1[1].astype(BF16), mlp_w2[1].astype(BF16), row(final_g), True)
    return x
```

```python
import functools
import math

import jax
import jax.numpy as jnp
from jax import lax
from jax.experimental import pallas as pl
from jax.experimental.pallas import tpu as pltpu

F32 = jnp.float32
BF16 = jnp.bfloat16
HIGHEST = lax.Precision.HIGHEST

V7X_LANES = 128
V7X_SUBLANES = 8
V7X_VMEM_LIMIT_BYTES = 56 * 1024 * 1024

EPS = 1e-6
GDN_HEADS = 8
GDN_DK = 128
GDN_CHUNK = 64
GDN_CHUNKS_PER_STEP = 2
CONV_K = 4
DIFF_HEADS = 4
DIFF_DH = 128
ROT_DIM = DIFF_DH // 4
ROPE_THETA = 500000.0
LOG2E = 1.4426950408889634

TS_PROJ = 512
TS_ATTN_PROJ = 1024
TS_SCAN = 512
TM_MLP = 1024
TQ_ATTN = 512
TK_ATTN = 2048
ATTN_ROW_TILE = 256
FF_CHUNK = 1024


def _params(sem):
    return pltpu.CompilerParams(dimension_semantics=sem, vmem_limit_bytes=V7X_VMEM_LIMIT_BYTES)


def _sigmoid(x):
    return 1.0 / (1.0 + jnp.exp(-x))


def _silu(x):
    return x * _sigmoid(x)


def _softplus(x):
    return jnp.maximum(x, 0.0) + jnp.log1p(jnp.exp(-jnp.abs(x)))


def _dot(a, b, precision=None):
    return jnp.dot(a, b, preferred_element_type=F32, precision=precision)


def _dot_nt(a, b):
    return lax.dot_general(a, b, (((1,), (1,)), ((), ())), preferred_element_type=F32)


def _dot_tn(a, b):
    return lax.dot_general(a, b, (((0,), (0,)), ((), ())), preferred_element_type=F32)


def _split_bf16(x):
    hi = x.astype(BF16)
    return hi, (x - hi.astype(F32)).astype(BF16)


def _dot3(a, b):
    a_hi, a_lo = _split_bf16(a)
    b_hi, b_lo = _split_bf16(b)
    return _dot(jnp.concatenate([a_hi, a_lo, a_hi], axis=1),
                jnp.concatenate([b_hi, b_hi, b_lo], axis=0))


def _mod_kernel(ct_ref, w_ref, b_ref, o_ref, *, batch):
    cs = _silu(ct_ref[...])
    w = w_ref[0]
    for b in range(batch):
        o_ref[0, b:b + 1, :] = jnp.sum(w * cs[:, b:b + 1], axis=0, keepdims=True) + b_ref[0]


def _mod_call(c_t, w, bias, tn=2048):
    n_l, d, n = w.shape
    batch = c_t.shape[1]
    return pl.pallas_call(
        functools.partial(_mod_kernel, batch=batch),
        grid=(n_l, n // tn),
        in_specs=[
            pl.BlockSpec((d, batch), lambda l, j: (0, 0)),
            pl.BlockSpec((1, d, tn), lambda l, j: (l, 0, j)),
            pl.BlockSpec((1, 1, tn), lambda l, j: (l, 0, j)),
        ],
        out_specs=pl.BlockSpec((1, batch, tn), lambda l, j: (l, 0, j)),
        out_shape=jax.ShapeDtypeStruct((n_l, batch, n), F32),
        compiler_params=_params(("arbitrary", "arbitrary")),
        name="mod_vectors",
    )(c_t, w, bias)


def _modulated_norm(x, g_row, shift_row, scale_row):
    rstd = lax.rsqrt(jnp.mean(x * x, axis=-1, keepdims=True) + EPS)
    return (x * rstd) * (g_row * (1.0 + scale_row)) + shift_row


def _gdn_inproj_kernel(x_ref, mod_ref, g_ref, w_ref, wab_ref, cw_ref, ad_ref,
                       o_ref, gb_ref, big_ref, *, ts):
    d = x_ref.shape[2]

    @pl.when(pl.program_id(1) == 0)
    def _():
        big_ref[:, ts:ts + V7X_SUBLANES] = jnp.zeros((3, V7X_SUBLANES, d), F32)

    m = mod_ref[0]
    hb = _modulated_norm(x_ref[0], g_ref[...], m[0:1], m[1:2]).astype(BF16)
    ab = _dot(hb, wab_ref[...])
    ad = ad_ref[...]
    g = -jnp.exp(ad[0:1]) * _softplus(ab + ad[1:2])
    lane = lax.broadcasted_iota(jnp.int32, ab.shape, 1)
    gb_ref[0] = jnp.where(lane < GDN_HEADS, g, _sigmoid(ab))

    base = V7X_SUBLANES - (CONV_K - 1)
    for j in range(3):
        cs_j = slice(j * d, (j + 1) * d)
        proj = _dot(hb, w_ref[:, cs_j])
        big_ref[j, 0:V7X_SUBLANES] = big_ref[j, ts:ts + V7X_SUBLANES]
        big_ref[j, V7X_SUBLANES:] = proj
        cw = cw_ref[:, cs_j]
        y = big_ref[j, base:base + ts] * cw[0:1]
        for t in range(1, CONV_K):
            y = y + big_ref[j, base + t:base + t + ts] * cw[t:t + 1]
        y = _silu(y)
        if j == 2:
            o_ref[0, :, cs_j] = y
        else:
            for hh in range(GDN_HEADS):
                cs = slice(hh * GDN_DK, (hh + 1) * GDN_DK)
                yh = y[:, cs]
                o_ref[0, :, j * d + hh * GDN_DK:j * d + (hh + 1) * GDN_DK] = (
                    yh * lax.rsqrt(jnp.sum(yh * yh, axis=-1, keepdims=True) + EPS))
    o_ref[0, :, 3 * d:4 * d] = _dot(hb, w_ref[:, 3 * d:4 * d])


def _gdn_inproj(x, mod, g_row, w, wab, cw, ad, ts=TS_PROJ):
    bsz, seq, d = x.shape
    n_out = w.shape[1]
    const = lambda shape: pl.BlockSpec(shape, lambda b, s: (0,) * len(shape),
                                       pipeline_mode=pl.Buffered(1))
    return pl.pallas_call(
        functools.partial(_gdn_inproj_kernel, ts=ts),
        grid=(bsz, seq // ts),
        in_specs=[
            pl.BlockSpec((1, ts, d), lambda b, s: (b, s, 0)),
            pl.BlockSpec((1, 6, d), lambda b, s: (b, 0, 0)),
            const((1, d)),
            const((d, n_out)),
            const((d, V7X_LANES)),
            const((CONV_K, 3 * d)),
            const((2, V7X_LANES)),
        ],
        out_specs=[
            pl.BlockSpec((1, ts, n_out), lambda b, s: (b, s, 0)),
            pl.BlockSpec((1, ts, V7X_LANES), lambda b, s: (b, s, 0)),
        ],
        out_shape=[
            jax.ShapeDtypeStruct((bsz, seq, n_out), F32),
            jax.ShapeDtypeStruct((bsz, seq, V7X_LANES), F32),
        ],
        scratch_shapes=[pltpu.VMEM((3, ts + V7X_SUBLANES, d), F32)],
        compiler_params=_params(("arbitrary", "arbitrary")),
        name="gdn_inproj",
    )(x, mod, g_row, w, wab, cw, ad)


def _gdn_scan_kernel(q_ref, k_ref, v_ref, z_ref, gb_ref, og_ref, y_ref, state_ref, *, ts):
    c_sz = GDN_CHUNK

    @pl.when(pl.program_id(1) == 0)
    def _():
        state_ref[...] = jnp.zeros_like(state_ref)

    row = lax.broadcasted_iota(jnp.int32, (c_sz, c_sz), 0)
    col = lax.broadcasted_iota(jnp.int32, (c_sz, c_sz), 1)
    incl = row >= col
    strict = row > col
    ltri = incl.astype(F32)
    eye = (row == col).astype(F32)
    og = og_ref[...]
    q_scale = GDN_DK ** -0.5

    heads = range(GDN_HEADS)
    cols = [slice(h * GDN_DK, (h + 1) * GDN_DK) for h in heads]
    n_sub = GDN_CHUNKS_PER_STEP

    def step(c, carry):
        rows, gbc, gcum, gcum_t = [], [], [], []
        for ci in range(n_sub):
            r0 = pl.multiple_of((c * n_sub + ci) * c_sz, c_sz)
            rows.append(pl.ds(r0, c_sz))
            gbc.append(gb_ref[0, rows[ci], :])
            gcum.append(_dot(ltri, gbc[ci], HIGHEST))
            gcum_t.append(gcum[ci].T)
        streams = [(ci, h) for ci in range(n_sub) for h in heads]
        qh = {s: q_ref[0, rows[s[0]], cols[s[1]]] * q_scale for s in streams}
        kh = {s: k_ref[0, rows[s[0]], cols[s[1]]] for s in streams}
        gc = {(ci, h): gcum[ci][:, h:h + 1] for ci, h in streams}
        beta = {(ci, h): gbc[ci][:, GDN_HEADS + h:GDN_HEADS + h + 1] for ci, h in streams}
        glast = {(ci, h): gcum[ci][c_sz - 1:c_sz, h:h + 1] for ci, h in streams}
        eg = {s: jnp.exp(gc[s]) for s in streams}
        decay = {(ci, h): jnp.where(incl, jnp.exp(jnp.where(incl, gc[(ci, h)] - gcum_t[ci][h:h + 1, :],
                                                           0.0)), 0.0) for ci, h in streams}
        kb = {s: kh[s] * beta[s] for s in streams}
        kk = {s: _dot_nt(jnp.concatenate([kb[s], qh[s]], axis=0).astype(BF16), kh[s].astype(BF16))
              for s in streams}
        a_mat = {s: jnp.where(strict, kk[s][:c_sz] * decay[s], 0.0) for s in streams}
        qk = {s: (kk[s][c_sz:] * decay[s]).astype(BF16) for s in streams}
        x_mat = {s: eye - a_mat[s] for s in streams}
        p_mat = {s: _dot3(a_mat[s], a_mat[s]) for s in streams}
        for _ in range(4):
            xp = {s: _dot3(jnp.concatenate([x_mat[s], p_mat[s]], axis=0), p_mat[s]) for s in streams}
            x_mat = {s: x_mat[s] + xp[s][:c_sz] for s in streams}
            p_mat = {s: xp[s][c_sz:] for s in streams}
        xl = {s: _dot3(x_mat[s], p_mat[s]) for s in streams}
        x_mat = {s: x_mat[s] + xl[s] for s in streams}
        sol = {s: _dot3(x_mat[s], jnp.concatenate([v_ref[0, rows[s[0]], cols[s[1]]] * beta[s],
                                                   kb[s] * eg[s]], axis=1))
               for s in streams}
        wq_lhs = {s: jnp.concatenate([sol[s][:, GDN_DK:], qh[s] * eg[s]], axis=0).astype(BF16)
                  for s in streams}
        kdec = {s: (kh[s] * jnp.exp(glast[s] - gc[s])).astype(BF16) for s in streams}
        for ci in range(n_sub):
            st = [state_ref[h] for h in heads]
            wq = [_dot(wq_lhs[(ci, h)], st[h].astype(BF16)) for h in heads]
            v_new = [(sol[(ci, h)][:, :GDN_DK] - wq[h][:c_sz]).astype(BF16) for h in heads]
            o = [wq[h][c_sz:] + _dot(qk[(ci, h)], v_new[h]) for h in heads]
            for h in heads:
                state_ref[h] = st[h] * jnp.exp(glast[(ci, h)]) + _dot_tn(kdec[(ci, h)], v_new[h])
            for h in heads:
                on = o[h] * lax.rsqrt(jnp.mean(o[h] * o[h], axis=-1, keepdims=True) + EPS) * og
                y_ref[0, rows[ci], cols[h]] = (on * _silu(z_ref[0, rows[ci], cols[h]])).astype(BF16)
        return carry

    lax.fori_loop(0, ts // (c_sz * n_sub), step, 0)


def _gdn_scan(qkvz, gb, og_row, ts=TS_SCAN):
    bsz, seq, _ = qkvz.shape
    d = GDN_HEADS * GDN_DK
    col = lambda j: pl.BlockSpec((1, ts, d), lambda b, s: (b, s, j))
    return pl.pallas_call(
        functools.partial(_gdn_scan_kernel, ts=ts),
        grid=(bsz, seq // ts),
        in_specs=[col(0), col(1), col(2), col(3),
                  pl.BlockSpec((1, ts, V7X_LANES), lambda b, s: (b, s, 0)),
                  pl.BlockSpec((1, GDN_DK), lambda b, s: (0, 0))],
        out_specs=pl.BlockSpec((1, ts, d), lambda b, s: (b, s, 0)),
        out_shape=jax.ShapeDtypeStruct((bsz, seq, d), BF16),
        scratch_shapes=[pltpu.VMEM((GDN_HEADS, GDN_DK, GDN_DK), F32)],
        compiler_params=_params(("arbitrary", "arbitrary")),
        name="gdn_scan",
    )(qkvz, qkvz, qkvz, qkvz, gb, og_row)


def _mixer_out_mlp_kernel(x_ref, y_ref, mod_ref, wo_ref, g_ref, w1_ref, w2_ref, fg_ref, o_ref,
                          *, final_norm):
    m = mod_ref[0]
    x1 = x_ref[0] + m[2:3] * _dot(y_ref[0], wo_ref[...])
    h = _modulated_norm(x1, g_ref[...], m[3:4], m[4:5]).astype(BF16)
    d_ff = w1_ref.shape[1]
    acc = None
    for c in range(d_ff // FF_CHUNK):
        cs = slice(c * FF_CHUNK, (c + 1) * FF_CHUNK)
        hid = jnp.maximum(_dot(h, w1_ref[:, cs]), 0.0)
        part = _dot((hid * hid).astype(BF16), w2_ref[cs, :])
        acc = part if acc is None else acc + part
    x2 = x1 + m[5:6] * acc
    if final_norm:
        x2 = x2 * lax.rsqrt(jnp.mean(x2 * x2, axis=-1, keepdims=True) + EPS) * fg_ref[...]
    o_ref[0] = x2


def _mixer_out_mlp(x, y, mod, wo, g_row, w1, w2, fg_row, final_norm, tm=TM_MLP):
    bsz, seq, d = x.shape
    d_ff = w1.shape[1]
    const = lambda shape: pl.BlockSpec(shape, lambda b, s: (0,) * len(shape),
                                       pipeline_mode=pl.Buffered(1))
    return pl.pallas_call(
        functools.partial(_mixer_out_mlp_kernel, final_norm=final_norm),
        grid=(bsz, seq // tm),
        in_specs=[
            pl.BlockSpec((1, tm, d), lambda b, s: (b, s, 0)),
            pl.BlockSpec((1, tm, d), lambda b, s: (b, s, 0)),
            pl.BlockSpec((1, 6, d), lambda b, s: (b, 0, 0)),
            const((d, d)),
            const((1, d)),
            const((d, d_ff)),
            const((d_ff, d)),
            const((1, d)),
        ],
        out_specs=pl.BlockSpec((1, tm, d), lambda b, s: (b, s, 0)),
        out_shape=jax.ShapeDtypeStruct((bsz, seq, d), F32),
        compiler_params=_params(("arbitrary", "arbitrary")),
        name="mixer_out_mlp",
    )(x, y, mod, wo, g_row, w1, w2, fg_row)


def _rope(y, cos_t, sin_t):
    outs = []
    for gidx in range(y.shape[1] // DIFF_DH):
        yh = y[:, gidx * DIFF_DH:(gidx + 1) * DIFF_DH]
        outs.append(yh * cos_t + pltpu.roll(yh, DIFF_DH // 2, 1) * sin_t)
    return outs


def _attn_inproj_kernel(x_ref, mod_ref, kvmod_ref, gq_ref, gkv_ref, w_ref, tile_cs_ref, row_cs_ref,
                        o_ref):
    d = x_ref.shape[2]
    x = x_ref[0]
    xn = x * lax.rsqrt(jnp.mean(x * x, axis=-1, keepdims=True) + EPS)
    m = mod_ref[0]
    km = kvmod_ref[0]
    hq = (xn * (gq_ref[...] * (1.0 + m[1:2])) + m[0:1]).astype(BF16)
    hk = (xn * (gkv_ref[...] * (1.0 + km[1:2])) + km[0:1]).astype(BF16)

    tc, tsn = tile_cs_ref[0, 0:1, :], tile_cs_ref[0, 1:2, :]
    rc, rsn = row_cs_ref[0], row_cs_ref[1]
    lane = lax.broadcasted_iota(jnp.int32, rc.shape, 1)
    cos_t = tc * rc - tsn * rsn
    sin_t = jnp.where(lane < DIFF_DH // 2, -1.0, 1.0) * (tsn * rc + tc * rsn)

    q_scale = (DIFF_DH ** -0.5) * LOG2E
    yq = _dot(hq, w_ref[:, 0:d])
    for gidx, yr in enumerate(_rope(yq, cos_t, sin_t)):
        o_ref[0, :, gidx * DIFF_DH:(gidx + 1) * DIFF_DH] = (yr * q_scale).astype(BF16)
    yk = _dot(hk, w_ref[:, d:2 * d])
    for gidx, yr in enumerate(_rope(yk, cos_t, sin_t)):
        o_ref[0, :, d + gidx * DIFF_DH:d + (gidx + 1) * DIFF_DH] = yr.astype(BF16)
    o_ref[0, :, 2 * d:3 * d] = _dot(hk, w_ref[:, 2 * d:3 * d]).astype(BF16)


def _attn_inproj(x, mod, kvmod, gq_row, gkv_row, w, tile_cs, row_cs, ts=TS_ATTN_PROJ):
    bsz, seq, d = x.shape
    n_out = w.shape[1]
    const = lambda shape: pl.BlockSpec(shape, lambda b, s: (0,) * len(shape),
                                       pipeline_mode=pl.Buffered(1))
    return pl.pallas_call(
        _attn_inproj_kernel,
        grid=(bsz, seq // ts),
        in_specs=[
            pl.BlockSpec((1, ts, d), lambda b, s: (b, s, 0)),
            pl.BlockSpec((1, 6, d), lambda b, s: (b, 0, 0)),
            pl.BlockSpec((1, 2, d), lambda b, s: (b, 0, 0)),
            const((1, d)),
            const((1, d)),
            const((d, n_out)),
            pl.BlockSpec((1, 2, DIFF_DH), lambda b, s: (s, 0, 0)),
            const((2, ts, DIFF_DH)),
        ],
        out_specs=pl.BlockSpec((1, ts, n_out), lambda b, s: (b, s, 0)),
        out_shape=jax.ShapeDtypeStruct((bsz, seq, n_out), BF16),
        compiler_params=_params(("arbitrary", "arbitrary")),
        name="attn_inproj",
    )(x, mod, kvmod, gq_row, gkv_row, w, tile_cs, row_cs)


def _diff_attn_kernel(q_ref, k_ref, v_ref, lam_ref, g_ref, o_ref, m_ref, l_ref, acc_ref,
                      s_ref, mx_ref, *, tq, tk, lam_init):
    qi = pl.program_id(2)
    m_ref[...] = jnp.full(m_ref.shape, -jnp.inf, F32)
    l_ref[...] = jnp.zeros_like(l_ref)
    acc_ref[...] = jnp.zeros_like(acc_ref)

    rt = ATTN_ROW_TILE
    n_rt = tq // rt

    def chunks(width):
        return [slice(j * V7X_LANES, (j + 1) * V7X_LANES) for j in range(width // V7X_LANES)]

    def scores_rows(kblk, slot, rq, width=None):
        rk = pl.ds(pl.multiple_of(kblk * tk, tk), tk if width is None else width)
        for mi in range(2):
            cs = slice(mi * DIFF_DH, (mi + 1) * DIFF_DH)
            s = _dot_nt(q_ref[0, rq, cs], k_ref[0, rk, cs])
            if width is None:
                s_ref[slot, mi, rq, :] = s
                mx_ref[slot, mi, rq, :] = functools.reduce(jnp.maximum, [s[:, c] for c in chunks(tk)])
            else:
                s_ref[slot, mi, rq, 0:width] = s

    def softmax_rows(slot, rq, width, diag_row0):
        if diag_row0 is not None:
            r = lax.broadcasted_iota(jnp.int32, (rt, V7X_LANES), 0)
            c = lax.broadcasted_iota(jnp.int32, (rt, V7X_LANES), 1)
            diff = r - c + diag_row0
        out = []
        for mi in range(2):
            sc = [s_ref[slot, mi, rq, c_] for c_ in chunks(width)]
            if diag_row0 is not None:
                sc = [jnp.where(diff >= c_.start, x, -jnp.inf) for c_, x in zip(chunks(width), sc)]
                mx = functools.reduce(jnp.maximum, sc)
            else:
                mx = mx_ref[slot, mi, rq, :]
            m_old = m_ref[mi, rq, :]
            m_new = jnp.maximum(m_old, jnp.max(mx, axis=-1, keepdims=True))
            alpha = jnp.exp2(m_old - m_new)
            m_ref[mi, rq, :] = m_new
            ps = [jnp.exp2(x - m_new) for x in sc]
            l_ref[mi, rq, :] = alpha * l_ref[mi, rq, :] + functools.reduce(jnp.add, ps)
            out.append((alpha, jnp.concatenate(ps, axis=1).astype(BF16)))
        return out

    def pv_rows(mi, rq, alpha, p, v):
        acc_ref[mi, rq, :] = jnp.concatenate([alpha, alpha], axis=1) * acc_ref[mi, rq, :] + _dot(p, v)

    def row_tile(t):
        return pl.ds(pl.multiple_of(t * rt, rt), rt)

    blocks_per_k = tk // tq
    n_full = qi // blocks_per_k
    rem = qi - n_full * blocks_per_k

    kdiag = pl.multiple_of(n_full * tk, tk)
    for rem_static in range(blocks_per_k):
        @pl.when(rem == rem_static)
        def _():
            for t in range(n_rt):
                scores_rows(n_full, 0, pl.ds(t * rt, rt), width=rem_static * tq + (t + 1) * rt)
            for t in range(n_rt):
                rq = pl.ds(t * rt, rt)
                first_visible = rem_static * tq + t * rt
                width = first_visible + rt
                v = v_ref[0, pl.ds(kdiag, width), :]
                for mi, (alpha, p) in enumerate(softmax_rows(0, rq, width, first_visible)):
                    pv_rows(mi, rq, alpha, p, v)
                scores_rows(0, 1, rq)

    def visible_block(i, produce_next):
        slot = (i + 1) & 1
        v = v_ref[0, pl.ds(pl.multiple_of(i * tk, tk), tk), :]

        def tile(t, c):
            rq = row_tile(t)
            for mi, (alpha, p) in enumerate(softmax_rows(slot, rq, tk, None)):
                pv_rows(mi, rq, alpha, p, v)
            if produce_next:
                scores_rows(i + 1, 1 - slot, rq)
            return c

        lax.fori_loop(0, n_rt, tile, 0)

    def visible_step(i, carry):
        visible_block(i, True)
        return carry

    lax.fori_loop(0, n_full - 1, visible_step, 0)

    @pl.when(n_full > 0)
    def _():
        visible_block(n_full - 1, False)

    lp = lam_ref[...]
    lam = (jnp.exp(jnp.sum(lp[0:1] * lp[1:2], axis=-1, keepdims=True))
           - jnp.exp(jnp.sum(lp[2:3] * lp[3:4], axis=-1, keepdims=True)) + lam_init)
    l0 = jnp.sum(l_ref[0], axis=-1, keepdims=True)
    l1 = jnp.sum(l_ref[1], axis=-1, keepdims=True)
    o = acc_ref[0] / l0 - lam * (acc_ref[1] / l1)
    on = o * lax.rsqrt(jnp.mean(o * o, axis=-1, keepdims=True) + EPS) * g_ref[...]
    o_ref[0] = (on * (1.0 - lam_init)).astype(BF16)


def _diff_attn(qkv, lam_params, g_row, lam_init, tq=TQ_ATTN, tk=TK_ATTN):
    bsz, seq, _ = qkv.shape
    hw = 2 * DIFF_DH
    assert tk % tq == 0 and seq % tk == 0
    return pl.pallas_call(
        functools.partial(_diff_attn_kernel, tq=tq, tk=tk, lam_init=lam_init),
        grid=(bsz, DIFF_HEADS, seq // tq),
        in_specs=[
            pl.BlockSpec((1, tq, hw), lambda b, h, i: (b, i, h)),
            pl.BlockSpec((1, seq, hw), lambda b, h, i: (b, 0, DIFF_HEADS + h)),
            pl.BlockSpec((1, seq, hw), lambda b, h, i: (b, 0, 2 * DIFF_HEADS + h),
                         pipeline_mode=pl.Buffered(1)),
            pl.BlockSpec((4, DIFF_DH), lambda b, h, i: (0, 0)),
            pl.BlockSpec((1, hw), lambda b, h, i: (0, 0)),
        ],
        out_specs=pl.BlockSpec((1, tq, hw), lambda b, h, i: (b, i, h)),
        out_shape=jax.ShapeDtypeStruct((bsz, seq, DIFF_HEADS * hw), BF16),
        scratch_shapes=[
            pltpu.VMEM((2, tq, V7X_LANES), F32),
            pltpu.VMEM((2, tq, V7X_LANES), F32),
            pltpu.VMEM((2, tq, hw), F32),
            pltpu.VMEM((2, 2, tq, tk), F32),
            pltpu.VMEM((2, 2, tq, V7X_LANES), F32),
        ],
        compiler_params=_params(("arbitrary", "arbitrary", "arbitrary")),
        name="diff_attn",
    )(qkv, qkv, qkv, lam_params, g_row)


def _rope_tables(seq, ts):
    inv_freq = ROPE_THETA ** (-jnp.arange(0, ROT_DIM, 2, dtype=F32) / ROT_DIM)
    pad = (DIFF_DH - ROT_DIM) // 2

    def lanes(pos):
        freqs = pos[:, None] * inv_freq[None, :]
        cos, sin = jnp.cos(freqs), jnp.sin(freqs)
        ones, zeros = jnp.ones((pos.shape[0], pad), F32), jnp.zeros((pos.shape[0], pad), F32)
        return (jnp.concatenate([cos, ones, cos, ones], axis=1),
                jnp.concatenate([sin, zeros, sin, zeros], axis=1))

    tile_cs = jnp.stack(lanes(jnp.arange(seq // ts, dtype=F32) * ts), axis=1)
    row_cs = jnp.stack(lanes(jnp.arange(ts, dtype=F32)), axis=0)
    return tile_cs, row_cs


def kernel(x, c, mod_w, mod_b, norm_mix_g, norm_mlp_g, a_w_in, a_conv_w, a_log, a_dt_bias, a_out_norm_g, a_w_out,
           kv_mod_w, kv_mod_b, kv_norm_g, kv_w, b_w_q, b_lambda, b_subln_g, b_w_out, mlp_w1, mlp_w2, final_g):
    bsz, seq, d = x.shape
    depth = mod_w.shape[0]
    assert depth == 2 and a_w_in.shape[0] == 1 and b_w_q.shape[0] == 1
    assert all(seq % t == 0 for t in (TS_PROJ, TS_ATTN_PROJ, TS_SCAN, TM_MLP, TQ_ATTN))

    c_t = c.T
    mod = _mod_call(c_t, mod_w, mod_b[:, None, :]).reshape(depth, bsz, 6, d)
    kvmod = _mod_call(c_t, kv_mod_w[None], kv_mod_b[None, None, :]).reshape(bsz, 2, d)

    row = lambda v: v.reshape(1, -1).astype(F32)

    w_in = a_w_in[0]
    n_main = 4 * d
    w_main = w_in[:, :n_main].astype(BF16)
    w_ab = jnp.pad(w_in[:, n_main:], ((0, 0), (0, V7X_LANES - 2 * GDN_HEADS))).astype(BF16)
    ad = jnp.pad(jnp.stack([a_log[0], a_dt_bias[0]]).astype(F32), ((0, 0), (0, V7X_LANES - GDN_HEADS)))
    qkvz, gb = _gdn_inproj(x, mod[0], row(norm_mix_g[0]), w_main, w_ab, a_conv_w[0].astype(F32), ad)
    y0 = _gdn_scan(qkvz, gb, row(a_out_norm_g[0]))
    x = _mixer_out_mlp(x, y0, mod[0], a_w_out[0].astype(BF16), row(norm_mlp_g[0]),
                       mlp_w1[0].astype(BF16), mlp_w2[0].astype(BF16), row(final_g), False)

    lam_init = 0.8 - 0.6 * math.exp(-0.3 * 1)

    def rope_order(w):
        half, mid = ROT_DIM // 2, DIFF_DH // 2
        g = w.reshape(d, -1, DIFF_DH)
        g = jnp.concatenate([g[..., :half], g[..., ROT_DIM:mid + half], g[..., half:ROT_DIM],
                             g[..., mid + half:]], axis=-1)
        return g.reshape(w.shape)

    n_k = 2 * DIFF_HEADS * DIFF_DH
    w_qkv = jnp.concatenate([rope_order(b_w_q[0]), rope_order(kv_w[:, :n_k]), kv_w[:, n_k:]],
                            axis=1).astype(BF16)
    tile_cs, row_cs = _rope_tables(seq, TS_ATTN_PROJ)
    qkv = _attn_inproj(x, mod[1], kvmod, row(norm_mix_g[1]), row(kv_norm_g), w_qkv, tile_cs, row_cs)
    y1 = _diff_attn(qkv, b_lambda[0].astype(F32), row(b_subln_g[0]), lam_init)
    x = _mixer_out_mlp(x, y1, mod[1], b_w_out[0].astype(BF16), row(norm_mlp_g[1]),
                       mlp_w1[1].astype(BF16), mlp_w2[1].astype(BF16), row(final_g), True)
    return x
```
